```python
import jax, jax.numpy as jnp
from jax import lax
import numpy as np

D_MODEL = 1024
BATCH = 32
SEQ = 2048
DEPTH = 2

N_MEM = 256
EPS = 1e-6
FOX_HEAD_DIM = 64
FOX_WIDTH = D_MODEL // 2
FOX_HEADS = FOX_WIDTH // FOX_HEAD_DIM
GMLP_GROUP_DIM = 64
GMLP_WIDTH = D_MODEL // 2
GMLP_GROUPS = GMLP_WIDTH // GMLP_GROUP_DIM
CHUNK = 128
Q_BLOCK = 128
MIX_WIDTH = FOX_WIDTH + GMLP_WIDTH
IN_WIDTH = 3 * FOX_WIDTH + FOX_HEADS + 2 * GMLP_WIDTH
CONV_WIDTH = D_MODEL
CONV_KERNEL = 31
XA_HEADS = 4
XA_HEAD_DIM = D_MODEL // XA_HEADS
FFN_HIDDEN = -(-8 * D_MODEL // (3 * 256)) * 256
N_EVEN = (DEPTH + 1) // 2
N_ODD = DEPTH // 2

kernel_name = "hybrid_gmlp_fox_conformer_memxattn"


def rmsnorm(x, g):
    x32 = x.astype(jnp.float32)
    y = x32 * lax.rsqrt(jnp.mean(x32 * x32, axis=-1, keepdims=True) + EPS)
    return (y * g.astype(jnp.float32)).astype(x.dtype)


def layernorm(x, g, b):
    x32 = x.astype(jnp.float32)
    mu = jnp.mean(x32, axis=-1, keepdims=True)
    xc = x32 - mu
    y = xc * lax.rsqrt(jnp.mean(xc * xc, axis=-1, keepdims=True) + EPS)
    return (y * g.astype(jnp.float32) + b.astype(jnp.float32)).astype(x.dtype)


def fox_attention(q, k, v, f_logit, f_bias):
    B, T, _ = q.shape
    scale = FOX_HEAD_DIM ** -0.5
    q = q.reshape(B, T, FOX_HEADS, FOX_HEAD_DIM) * scale
    k = k.reshape(B, T, FOX_HEADS, FOX_HEAD_DIM)
    v = v.reshape(B, T, FOX_HEADS, FOX_HEAD_DIM)
    log_f = jax.nn.log_sigmoid((f_logit + f_bias).astype(jnp.float32))
    cum = jnp.cumsum(log_f, axis=1).transpose(0, 2, 1)
    outs = []
    for i in range(T // Q_BLOCK):
        q0 = i * Q_BLOCK
        q1 = q0 + Q_BLOCK
        s = jnp.einsum('bqhd,bkhd->bhqk', q[:, q0:q1], k[:, :q1]).astype(jnp.float32)
        s = s + cum[:, :, q0:q1, None] - cum[:, :, None, :q1]
        causal = (q0 + jnp.arange(Q_BLOCK))[:, None] >= jnp.arange(q1)[None, :]
        p = jax.nn.softmax(jnp.where(causal, s, -jnp.inf), axis=-1).astype(v.dtype)
        outs.append(jnp.einsum('bhqk,bkhd->bqhd', p, v[:, :q1]))
    return jnp.concatenate(outs, axis=1).reshape(B, T, FOX_WIDTH)


def gmlp_spatial_gate(z, ln_g, ln_b, w_s, b_s):
    B, T, _ = z.shape
    z = jax.nn.gelu(z)
    u, vg = jnp.split(z, 2, axis=-1)
    vg = layernorm(vg, ln_g, ln_b)
    vg = vg.reshape(B, T // CHUNK, CHUNK, GMLP_GROUPS, GMLP_GROUP_DIM)
    w = w_s * jnp.tril(jnp.ones((CHUNK, CHUNK), dtype=w_s.dtype))
    mixed = jnp.einsum('gts,bcsgd->bctgd', w, vg) + b_s.T[:, :, None]
    return u * mixed.reshape(B, T, GMLP_WIDTH)


def even_mixer(h, w_in, f_bias, ln_g, ln_b, w_s, b_s, w_out):
    proj = h @ w_in
    F = FOX_WIDTH
    q, k, v, f_logit, z = jnp.split(proj, [F, 2 * F, 3 * F, 3 * F + FOX_HEADS], axis=-1)
    a_out = gmlp_spatial_gate(z, ln_g, ln_b, w_s, b_s)
    b_out = fox_attention(q, k, v, f_logit, f_bias)
    return jnp.concatenate([b_out, a_out], axis=-1) @ w_out


def conformer_conv(h, w_in, b_in, dw_w, dw_b, ln_g, ln_b, w_out, b_out):
    a, g = jnp.split(h @ w_in + b_in, 2, axis=-1)
    y = a * jax.nn.sigmoid(g)
    y = lax.conv_general_dilated(
        y, dw_w[:, None, :].astype(y.dtype), window_strides=(1,),
        padding=[(CONV_KERNEL - 1, 0)], dimension_numbers=('NWC', 'WIO', 'NWC'),
        feature_group_count=CONV_WIDTH) + dw_b
    y = jax.nn.silu(layernorm(y, ln_g, ln_b))
    return y @ w_out + b_out


def memory_cross_attention(h, m, wq, wkv, wo):
    B, T, _ = h.shape
    q = (h @ wq).reshape(B, T, XA_HEADS, XA_HEAD_DIM) * (XA_HEAD_DIM ** -0.5)
    k, v = jnp.split(m @ wkv, 2, axis=-1)
    k = k.reshape(B, -1, XA_HEADS, XA_HEAD_DIM)
    v = v.reshape(B, -1, XA_HEADS, XA_HEAD_DIM)
    s = jnp.einsum('bthd,bmhd->bhtm', q, k).astype(jnp.float32)
    p = jax.nn.softmax(s, axis=-1).astype(v.dtype)
    o = jnp.einsum('bhtm,bmhd->bthd', p, v).reshape(B, T, D_MODEL)
    return o @ wo


def swiglu(h, w_gu, w_down):
    g, u = jnp.split(h @ w_gu, 2, axis=-1)
    return (jax.nn.silu(g) * u) @ w_down


def setup_inputs(seed: int = 0) -> dict:
    key = jax.random.key(seed)
    ks = iter(jax.random.split(key, 40))

    def nrm(shape, scale):
        return jax.random.normal(next(ks), shape, jnp.float32) * scale

    def gain(shape):
        return 1.0 + nrm(shape, 0.02)

    D = D_MODEL
    return {
        "x": nrm((BATCH, SEQ, D), 1.0),
        "mem": nrm((BATCH, N_MEM, D), 1.0),
        "mix_norm_e": gain((N_EVEN, D)),
        "w_in_e": nrm((N_EVEN, D, IN_WIDTH), D ** -0.5),
        "fox_f_bias": 2.0 + nrm((N_EVEN, FOX_HEADS), 0.5),
        "gmlp_ln_g": gain((N_EVEN, GMLP_WIDTH)),
        "gmlp_ln_b": nrm((N_EVEN, GMLP_WIDTH), 0.02),
        "gmlp_w_s": nrm((N_EVEN, GMLP_GROUPS, CHUNK, CHUNK), CHUNK ** -0.5),
        "gmlp_b_s": gain((N_EVEN, GMLP_GROUPS, CHUNK)),
        "w_out_e": nrm((N_EVEN, MIX_WIDTH, D), MIX_WIDTH ** -0.5),
        "mix_norm_o": gain((N_ODD, D)),
        "conv_w_in": nrm((N_ODD, D, 2 * CONV_WIDTH), D ** -0.5),
        "conv_b_in": nrm((N_ODD, 2 * CONV_WIDTH), 0.02),
        "conv_dw_w": nrm((N_ODD, CONV_KERNEL, CONV_WIDTH), CONV_KERNEL ** -0.5),
        "conv_dw_b": nrm((N_ODD, CONV_WIDTH), 0.02),
        "conv_ln_g": gain((N_ODD, CONV_WIDTH)),
        "conv_ln_b": nrm((N_ODD, CONV_WIDTH), 0.02),
        "conv_w_out": nrm((N_ODD, CONV_WIDTH, D), CONV_WIDTH ** -0.5),
        "conv_b_out": nrm((N_ODD, D), 0.02),
        "xa_norm": gain((DEPTH, D)),
        "mem_norm": gain((DEPTH, D)),
        "xa_wq": nrm((DEPTH, D, D), D ** -0.5),
        "xa_wkv": nrm((DEPTH, D, 2 * D), D ** -0.5),
        "xa_wo": nrm((DEPTH, D, D), D ** -0.5),
        "ffn_norm": gain((DEPTH, D)),
        "ffn_w_gu": nrm((DEPTH, D, 2 * FFN_HIDDEN), D ** -0.5),
        "ffn_w_down": nrm((DEPTH, FFN_HIDDEN, D), FFN_HIDDEN ** -0.5),
        "final_norm": gain((D,)),
    }


def reference(x, mem, mix_norm_e, w_in_e, fox_f_bias, gmlp_ln_g, gmlp_ln_b, gmlp_w_s,
              gmlp_b_s, w_out_e, mix_norm_o, conv_w_in, conv_b_in, conv_dw_w, conv_dw_b,
              conv_ln_g, conv_ln_b, conv_w_out, conv_b_out, xa_norm, mem_norm, xa_wq,
              xa_wkv, xa_wo, ffn_norm, ffn_w_gu, ffn_w_down, final_norm):
    for layer in range(DEPTH):
        li = layer // 2
        if layer % 2 == 0:
            h = rmsnorm(x, mix_norm_e[li])
            x = x + even_mixer(h, w_in_e[li], fox_f_bias[li], gmlp_ln_g[li], gmlp_ln_b[li],
                               gmlp_w_s[li], gmlp_b_s[li], w_out_e[li])
        else:
            h = rmsnorm(x, mix_norm_o[li])
            x = x + conformer_conv(h, conv_w_in[li], conv_b_in[li], conv_dw_w[li],
                                   conv_dw_b[li], conv_ln_g[li], conv_ln_b[li],
                                   conv_w_out[li], conv_b_out[li])
        h = rmsnorm(x, xa_norm[layer])
        m = rmsnorm(mem, mem_norm[layer])
        x = x + memory_cross_attention(h, m, xa_wq[layer], xa_wkv[layer], xa_wo[layer])
        h = rmsnorm(x, ffn_norm[layer])
        x = x + swiglu(h, ffn_w_gu[layer], ffn_w_down[layer])
    return rmsnorm(x, final_norm)
```

```python
import functools

import jax
import jax.numpy as jnp
from jax import lax
from jax.experimental import pallas as pl
from jax.experimental.pallas import tpu as pltpu

F32 = jnp.float32
BF16 = jnp.bfloat16
EPS = 1e-6

LANES_V7X = 128
VMEM_LIMIT_V7X = 56 * 1024 * 1024

FOX_HEAD_DIM = 64
GMLP_GROUP_DIM = 64
GMLP_CHUNK = 128
XA_HEADS = 4
CONV_HALO = 32

TOKEN_TILE = 512
FOX_BLOCK = 256
FFN_CHUNKS = 2


def _rmsnorm(x, g):
    return x * lax.rsqrt(jnp.mean(x * x, axis=-1, keepdims=True) + EPS) * g


def _layernorm(x, g, b):
    mu = jnp.mean(x, axis=-1, keepdims=True)
    xc = x - mu
    return xc * lax.rsqrt(jnp.mean(xc * xc, axis=-1, keepdims=True) + EPS) * g + b


def _mm(a, b):
    return jnp.dot(a, b, preferred_element_type=F32)


def _full(a):
    nd = a.ndim
    return pl.BlockSpec(a.shape, lambda *_: (0,) * nd)


def _params(n_grid):
    return pltpu.CompilerParams(
        dimension_semantics=("arbitrary",) * n_grid, vmem_limit_bytes=VMEM_LIMIT_V7X)


def _even_in_kernel(x_ref, g_ref, wqkv_ref, wf_ref, fb_ref, wz_ref, lng_ref, lnb_ref,
                    ws_ref, bs_ref, q_ref, k_ref, v_ref, lf_ref, a_ref, *, heads):
    x = x_ref[...]
    tm = x.shape[0]
    h = _rmsnorm(x, g_ref[...]).astype(BF16)

    qkv = _mm(h, wqkv_ref[...])
    width = heads * FOX_HEAD_DIM
    scale = FOX_HEAD_DIM ** -0.5
    for hd in range(heads):
        lo = hd * FOX_HEAD_DIM
        q_ref[hd] = (qkv[:, lo:lo + FOX_HEAD_DIM] * scale).astype(BF16)
        k_ref[hd] = qkv[:, width + lo:width + lo + FOX_HEAD_DIM].astype(BF16)
        v_ref[hd] = qkv[:, 2 * width + lo:2 * width + lo + FOX_HEAD_DIM].astype(BF16)

    lf_ref[...] = jax.nn.log_sigmoid(_mm(h, wf_ref[...]) + fb_ref[...])

    z = jax.nn.gelu(_mm(h, wz_ref[...]))
    gw = z.shape[1] // 2
    u = z[:, :gw]
    vg = _layernorm(z[:, gw:], lng_ref[...], lnb_ref[...]).astype(BF16)

    groups = gw // GMLP_GROUP_DIM
    row = lax.broadcasted_iota(jnp.int32, (GMLP_CHUNK, GMLP_CHUNK), 0)
    col = lax.broadcasted_iota(jnp.int32, (GMLP_CHUNK, GMLP_CHUNK), 1)
    causal = row >= col
    w = [jnp.where(causal, ws_ref[g], 0.0).astype(BF16) for g in range(groups)]
    first_half = lax.broadcasted_iota(jnp.int32, (GMLP_CHUNK, 2 * GMLP_GROUP_DIM), 1) < GMLP_GROUP_DIM
    bs = bs_ref[...]
    for c in range(tm // GMLP_CHUNK):
        r0 = c * GMLP_CHUNK
        pieces = []
        for p in range(groups // 2):
            vgp = vg[r0:r0 + GMLP_CHUNK, p * 2 * GMLP_GROUP_DIM:(p + 1) * 2 * GMLP_GROUP_DIM]
            pieces.append(jnp.where(first_half, _mm(w[2 * p], vgp), _mm(w[2 * p + 1], vgp)))
        mixed = jnp.concatenate(pieces, axis=1) + bs
        a_ref[r0:r0 + GMLP_CHUNK, :] = (u[r0:r0 + GMLP_CHUNK] * mixed).astype(BF16)


def _even_in(x, g, wqkv, wf, fb, wz, lng, lnb, ws, bs, *, heads):
    B, T, D = x.shape
    tm = TOKEN_TILE
    gw = wz.shape[1] // 2
    tile = lambda w_: pl.BlockSpec((None, tm, w_), lambda b, t: (b, t, 0))
    head_tile = pl.BlockSpec((None, heads, tm, FOX_HEAD_DIM), lambda b, t: (b, 0, t, 0))
    head_shape = jax.ShapeDtypeStruct((B, heads, T, FOX_HEAD_DIM), BF16)
    return pl.pallas_call(
        functools.partial(_even_in_kernel, heads=heads),
        grid=(B, T // tm),
        in_specs=[tile(D), _full(g), _full(wqkv), _full(wf), _full(fb), _full(wz), _full(lng),
                  _full(lnb), _full(ws), _full(bs)],
        out_specs=[head_tile, head_tile, head_tile, tile(LANES_V7X), tile(gw)],
        out_shape=[head_shape, head_shape, head_shape,
                   jax.ShapeDtypeStruct((B, T, LANES_V7X), F32),
                   jax.ShapeDtypeStruct((B, T, gw), BF16)],
        compiler_params=_params(2),
        name="even_in",
    )(x, g, wqkv, wf, fb, wz, lng, lnb, ws, bs)


def _fox_cum_kernel(lf_ref, ct_ref, ch_ref, *, heads):
    x = lf_ref[...]
    T = x.shape[0]
    row = lax.broadcasted_iota(jnp.int32, x.shape, 0)
    shift = 1
    while shift < T:
        x = x + jnp.where(row >= shift, pltpu.roll(x, shift, axis=0), 0.0)
        shift *= 2
    ct_ref[...] = x
    ch_ref[...] = x.T[:heads]


def _fox_cum(lf, *, heads):
    B, T, L = lf.shape
    return pl.pallas_call(
        functools.partial(_fox_cum_kernel, heads=heads),
        grid=(B,),
        in_specs=[pl.BlockSpec((None, T, L), lambda b: (b, 0, 0))],
        out_specs=[pl.BlockSpec((None, T, L), lambda b: (b, 0, 0)),
                   pl.BlockSpec((None, heads, T), lambda b: (b, 0, 0))],
        out_shape=[jax.ShapeDtypeStruct((B, T, L), F32), jax.ShapeDtypeStruct((B, heads, T), F32)],
        compiler_params=_params(1),
        name="fox_cum",
    )(lf)


def _fox_attn_kernel(q_ref, k_ref, v_ref, ct_ref, ch_ref, o_ref):
    hp = pl.program_id(1)
    qi = pl.program_id(2)
    tq = q_ref.shape[1]
    ct = ct_ref[...]
    lane = lax.broadcasted_iota(jnp.int32, ct.shape, 1)
    row = lax.broadcasted_iota(jnp.int32, (tq, tq), 0)
    col = lax.broadcasted_iota(jnp.int32, (tq, tq), 1)
    causal = row >= col

    outs = []
    for j in range(2):
        cq = jnp.sum(jnp.where(lane == 2 * hp + j, ct, 0.0), axis=1, keepdims=True)
        q = q_ref[j]

        def step(kj, carry, masked, cq=cq, q=q, j=j):
            m, l, acc = carry
            s = lax.dot_general(q, k_ref[j, kj], (((1,), (1,)), ((), ())),
                                preferred_element_type=F32)
            s = s + (cq - ch_ref[j, kj])
            if masked:
                s = jnp.where(causal, s, -jnp.inf)
            m_new = jnp.maximum(m, jnp.max(s, axis=1, keepdims=True))
            alpha = jnp.exp(m - m_new)
            p = jnp.exp(s - m_new)
            l = alpha * l + jnp.sum(p, axis=1, keepdims=True)
            acc = alpha * acc + _mm(p.astype(BF16), v_ref[j, kj])
            return m_new, l, acc

        init = (jnp.full((tq, 1), -jnp.inf, F32), jnp.zeros((tq, 1), F32),
                jnp.zeros((tq, FOX_HEAD_DIM), F32))
        carry = lax.fori_loop(0, qi, functools.partial(step, masked=False), init)
        _, l, acc = step(qi, carry, masked=True)
        outs.append(acc / l)
    o_ref[...] = jnp.concatenate(outs, axis=1).astype(BF16)


def _fox_attn(q, k, v, ct, ch):
    B, H, T, hd = q.shape
    tq = FOX_BLOCK
    nk = T // tq
    k5 = k.reshape(B, H, nk, tq, hd)
    v5 = v.reshape(B, H, nk, tq, hd)
    ch5 = ch.reshape(B, H, nk, 1, tq)
    kv_spec = pl.BlockSpec((None, 2, nk, tq, hd), lambda b, h, i: (b, h, 0, 0, 0))
    return pl.pallas_call(
        _fox_attn_kernel,
        grid=(B, H // 2, nk),
        in_specs=[pl.BlockSpec((None, 2, tq, hd), lambda b, h, i: (b, h, i, 0)),
                  kv_spec, kv_spec,
                  pl.BlockSpec((None, tq, ct.shape[2]), lambda b, h, i: (b, i, 0)),
                  pl.BlockSpec((None, 2, nk, 1, tq), lambda b, h, i: (b, h, 0, 0, 0))],
        out_specs=pl.BlockSpec((None, tq, 2 * hd), lambda b, h, i: (b, i, h)),
        out_shape=jax.ShapeDtypeStruct((B, T, H * hd), BF16),
        compiler_params=_params(3),
        name="fox_attn",
    )(q, k5, v5, ct, ch5)


def _mem_kv_kernel(m_ref, g_ref, wkv_ref, kt_ref, v_ref):
    m = _rmsnorm(m_ref[...], g_ref[...]).astype(BF16)
    kv = _mm(m, wkv_ref[...])
    d = kv.shape[1] // 2
    kt_ref[...] = kv[:, :d].T.astype(BF16)
    v_ref[...] = kv[:, d:].astype(BF16)


def _mem_kv(mem, g, wkv):
    B, M, D = mem.shape
    depth = wkv.shape[0]
    return pl.pallas_call(
        _mem_kv_kernel,
        grid=(depth, B),
        in_specs=[pl.BlockSpec((None, M, D), lambda l, b: (b, 0, 0)),
                  pl.BlockSpec((None, 1, D), lambda l, b: (l, 0, 0)),
                  pl.BlockSpec((None, D, 2 * D), lambda l, b: (l, 0, 0))],
        out_specs=[pl.BlockSpec((None, None, D, M), lambda l, b: (l, b, 0, 0)),
                   pl.BlockSpec((None, None, M, D), lambda l, b: (l, b, 0, 0))],
        out_shape=[jax.ShapeDtypeStruct((depth, B, D, M), BF16),
                   jax.ShapeDtypeStruct((depth, B, M, D), BF16)],
        compiler_params=_params(2),
        name="mem_kv",
    )(mem, g, wkv)


def _xattn_body(x, g_ref, wq_ref, kt_ref, v_ref, wo_ref):
    hd = wq_ref.shape[1] // XA_HEADS
    h = _rmsnorm(x, g_ref[...]).astype(BF16)
    q = (_mm(h, wq_ref[...]) * hd ** -0.5).astype(BF16)
    outs = []
    for hh in range(XA_HEADS):
        lo = hh * hd
        s = _mm(q[:, lo:lo + hd], kt_ref[lo:lo + hd, :])
        p = jnp.exp(s - jnp.max(s, axis=1, keepdims=True))
        p = (p / jnp.sum(p, axis=1, keepdims=True)).astype(BF16)
        outs.append(_mm(p, v_ref[:, lo:lo + hd]).astype(BF16))
    return x + _mm(jnp.concatenate(outs, axis=1), wo_ref[...])


def _xattn_kernel(x_ref, g_ref, wq_ref, kt_ref, v_ref, wo_ref, o_ref):
    o_ref[...] = _xattn_body(x_ref[...], g_ref, wq_ref, kt_ref, v_ref, wo_ref)


def _mix_xattn_kernel(x_ref, b_ref, a_ref, wout_ref, g_ref, wq_ref, kt_ref, v_ref, wo_ref, o_ref):
    fw = b_ref.shape[1]
    x = x_ref[...] + _mm(b_ref[...], wout_ref[:fw, :]) + _mm(a_ref[...], wout_ref[fw:, :])
    o_ref[...] = _xattn_body(x, g_ref, wq_ref, kt_ref, v_ref, wo_ref)


def _xattn(x, mix, g, wq, kt, v, wo):
    B, T, D = x.shape
    tm = TOKEN_TILE
    M = v.shape[1]
    tile = lambda w_: pl.BlockSpec((None, tm, w_), lambda b, t: (b, t, 0))
    kv_specs = [pl.BlockSpec((None, D, M), lambda b, t: (b, 0, 0)),
                pl.BlockSpec((None, M, D), lambda b, t: (b, 0, 0))]
    if mix is None:
        kern, pre_args, pre_specs = _xattn_kernel, (), []
    else:
        b_out, a_out, w_out = mix
        kern, pre_args = _mix_xattn_kernel, (b_out, a_out, w_out)
        pre_specs = [tile(b_out.shape[2]), tile(a_out.shape[2]), _full(w_out)]
    return pl.pallas_call(
        kern,
        grid=(B, T // tm),
        in_specs=[tile(D)] + pre_specs + [_full(g), _full(wq)] + kv_specs + [_full(wo)],
        out_specs=tile(D),
        out_shape=jax.ShapeDtypeStruct((B, T, D), F32),
        compiler_params=_params(2),
        name="xattn" if mix is None else "mix_xattn",
    )(x, *pre_args, g, wq, kt, v, wo)


def _ffn_kernel(x_ref, g_ref, wg_ref, wu_ref, wd_ref, fg_ref, o_ref, *, final):
    x = x_ref[...]
    h = _rmsnorm(x, g_ref[...]).astype(BF16)
    hc = wg_ref.shape[1] // FFN_CHUNKS
    y = x
    for c in range(FFN_CHUNKS):
        lo = c * hc
        a = jax.nn.silu(_mm(h, wg_ref[:, lo:lo + hc])) * _mm(h, wu_ref[:, lo:lo + hc])
        y = y + _mm(a.astype(BF16), wd_ref[lo:lo + hc, :])
    if final:
        y = _rmsnorm(y, fg_ref[...])
    o_ref[...] = y


def _ffn(x, g, wg, wu, wd, fg, *, final):
    B, T, D = x.shape
    tm = TOKEN_TILE
    tile = pl.BlockSpec((None, tm, D), lambda b, t: (b, t, 0))
    return pl.pallas_call(
        functools.partial(_ffn_kernel, final=final),
        grid=(B, T // tm),
        in_specs=[tile, _full(g), _full(wg), _full(wu), _full(wd), _full(fg)],
        out_specs=tile,
        out_shape=jax.ShapeDtypeStruct((B, T, D), F32),
        compiler_params=_params(2),
        name="ffn_final" if final else "ffn",
    )(x, g, wg, wu, wd, fg)


def _conv_kernel(x_ref, g_ref, win_ref, bin_ref, dww_ref, dwb_ref, lng_ref, lnb_ref, wout_ref,
                 bout_ref, o_ref, y_buf):
    x = x_ref[...]
    tm = x.shape[0]
    taps = dww_ref.shape[0]

    @pl.when(pl.program_id(1) == 0)
    def _():
        y_buf[:CONV_HALO, :] = jnp.zeros((CONV_HALO, y_buf.shape[1]), F32)

    h = _rmsnorm(x, g_ref[...]).astype(BF16)
    ag = _mm(h, win_ref[...]) + bin_ref[...]
    cw = ag.shape[1] // 2
    y_buf[CONV_HALO:, :] = ag[:, :cw] * jax.nn.sigmoid(ag[:, cw:])

    base = CONV_HALO - (taps - 1)
    acc = jnp.zeros((tm, cw), F32)
    for j in range(taps):
        acc = acc + dww_ref[j:j + 1, :] * y_buf[base + j:base + j + tm, :]
    y_buf[:CONV_HALO, :] = y_buf[tm:tm + CONV_HALO, :]

    y = jax.nn.silu(_layernorm(acc + dwb_ref[...], lng_ref[...], lnb_ref[...]))
    o_ref[...] = x + _mm(y.astype(BF16), wout_ref[...]) + bout_ref[...]


def _conv(x, g, win, bin_, dww, dwb, lng, lnb, wout, bout):
    B, T, D = x.shape
    tm = TOKEN_TILE
    cw = wout.shape[0]
    tile = pl.BlockSpec((None, tm, D), lambda b, t: (b, t, 0))
    return pl.pallas_call(
        _conv_kernel,
        grid=(B, T // tm),
        in_specs=[tile, _full(g), _full(win), _full(bin_), _full(dww), _full(dwb), _full(lng),
                  _full(lnb), _full(wout), _full(bout)],
        out_specs=tile,
        out_shape=jax.ShapeDtypeStruct((B, T, D), F32),
        scratch_shapes=[pltpu.VMEM((CONV_HALO + tm, cw), F32)],
        compiler_params=_params(2),
        name="conv",
    )(x, g, win, bin_, dww, dwb, lng, lnb, wout, bout)


def kernel(x, mem, mix_norm_e, w_in_e, fox_f_bias, gmlp_ln_g, gmlp_ln_b, gmlp_w_s, gmlp_b_s, w_out_e, mix_norm_o, conv_w_in, conv_b_in, conv_dw_w, conv_dw_b, conv_ln_g, conv_ln_b, conv_w_out, conv_b_out, xa_norm, mem_norm, xa_wq, xa_wkv, xa_wo, ffn_norm, ffn_w_gu, ffn_w_down, final_norm):
    D = x.shape[2]
    depth = xa_wq.shape[0]
    heads = fox_f_bias.shape[1]
    fw = heads * FOX_HEAD_DIM
    groups = gmlp_w_s.shape[1]
    gw = groups * GMLP_GROUP_DIM
    hidden = ffn_w_down.shape[1]
    assert 2 * GMLP_GROUP_DIM == LANES_V7X and groups % 2 == 0 and heads % 2 == 0
    assert w_in_e.shape[2] == 3 * fw + heads + 2 * gw and conv_dw_w.shape[1] <= CONV_HALO + 1

    row = lambda p: p.reshape(1, -1)
    kt, vm = _mem_kv(mem, mem_norm.reshape(depth, 1, D), xa_wkv.astype(BF16))

    for layer in range(depth):
        li = layer // 2
        if layer % 2 == 0:
            w_in = w_in_e[li]
            wf = jnp.pad(w_in[:, 3 * fw:3 * fw + heads], ((0, 0), (0, LANES_V7X - heads)))
            fb = jnp.pad(fox_f_bias[li], (0, LANES_V7X - heads)).reshape(1, -1)
            bs = jnp.repeat(gmlp_b_s[li].T, GMLP_GROUP_DIM, axis=1)
            q, k, v, lf, a_out = _even_in(
                x, row(mix_norm_e[li]), w_in[:, :3 * fw].astype(BF16), wf.astype(BF16), fb,
                w_in[:, 3 * fw + heads:].astype(BF16), row(gmlp_ln_g[li]), row(gmlp_ln_b[li]),
                gmlp_w_s[li], bs, heads=heads)
            ct, ch = _fox_cum(lf, heads=heads)
            b_out = _fox_attn(q, k, v, ct, ch)
            mix = (b_out, a_out, w_out_e[li].astype(BF16))
        else:
            x = _conv(x, row(mix_norm_o[li]), conv_w_in[li].astype(BF16), row(conv_b_in[li]),
                      conv_dw_w[li], row(conv_dw_b[li]), row(conv_ln_g[li]), row(conv_ln_b[li]),
                      conv_w_out[li].astype(BF16), row(conv_b_out[li]))
            mix = None
        x = _xattn(x, mix, row(xa_norm[layer]), xa_wq[layer].astype(BF16), kt[layer], vm[layer],
                   xa_wo[layer].astype(BF16))
        w_gu = ffn_w_gu[layer].astype(BF16)
        x = _ffn(x, row(ffn_norm[layer]), w_gu[:, :hidden], w_gu[:, hidden:],
                 ffn_w_down[layer].astype(BF16), row(final_norm), final=layer == depth - 1)
    return x
```

```python
import functools

import jax
import jax.numpy as jnp
from jax import lax
from jax.experimental import pallas as pl
from jax.experimental.pallas import tpu as pltpu

F32 = jnp.float32
BF16 = jnp.bfloat16
EPS = 1e-6

LANES_V7X = 128
VMEM_LIMIT_V7X = 56 * 1024 * 1024

FOX_HEAD_DIM = 64
GMLP_GROUP_DIM = 64
GMLP_CHUNK = 128
XA_HEADS = 4
CONV_HALO = 32

TOKEN_TILE = 512
FOX_BLOCK = 256
FFN_CHUNKS = 2


def _rmsnorm(x, g):
    return x * lax.rsqrt(jnp.mean(x * x, axis=-1, keepdims=True) + EPS) * g


def _layernorm(x, g, b):
    mu = jnp.mean(x, axis=-1, keepdims=True)
    xc = x - mu
    return xc * lax.rsqrt(jnp.mean(xc * xc, axis=-1, keepdims=True) + EPS) * g + b


def _mm(a, b):
    return jnp.dot(a, b, preferred_element_type=F32)


def _full(a):
    nd = a.ndim
    return pl.BlockSpec(a.shape, lambda *_: (0,) * nd)


def _params(n_grid):
    return pltpu.CompilerParams(
        dimension_semantics=("arbitrary",) * n_grid, vmem_limit_bytes=VMEM_LIMIT_V7X)


def _even_in_kernel(x_ref, g_ref, wqkv_ref, wf_ref, fb_ref, wz_ref, lng_ref, lnb_ref,
                    ws_ref, bs_ref, q_ref, k_ref, v_ref, lf_ref, a_ref, *, heads):
    x = x_ref[...]
    tm = x.shape[0]
    h = _rmsnorm(x, g_ref[...]).astype(BF16)

    qkv = _mm(h, wqkv_ref[...])
    width = heads * FOX_HEAD_DIM
    scale = FOX_HEAD_DIM ** -0.5
    for hd in range(heads):
        lo = hd * FOX_HEAD_DIM
        q_ref[hd] = (qkv[:, lo:lo + FOX_HEAD_DIM] * scale).astype(BF16)
        k_ref[hd] = qkv[:, width + lo:width + lo + FOX_HEAD_DIM].astype(BF16)
        v_ref[hd] = qkv[:, 2 * width + lo:2 * width + lo + FOX_HEAD_DIM].astype(BF16)

    lf_ref[...] = jax.nn.log_sigmoid(_mm(h, wf_ref[...]) + fb_ref[...])

    z = jax.nn.gelu(_mm(h, wz_ref[...]))
    gw = z.shape[1] // 2
    u = z[:, :gw]
    vg = _layernorm(z[:, gw:], lng_ref[...], lnb_ref[...]).astype(BF16)

    groups = gw // GMLP_GROUP_DIM
    row = lax.broadcasted_iota(jnp.int32, (GMLP_CHUNK, GMLP_CHUNK), 0)
    col = lax.broadcasted_iota(jnp.int32, (GMLP_CHUNK, GMLP_CHUNK), 1)
    causal = row >= col
    w = [jnp.where(causal, ws_ref[g], 0.0).astype(BF16) for g in range(groups)]
    first_half = lax.broadcasted_iota(jnp.int32, (GMLP_CHUNK, 2 * GMLP_GROUP_DIM), 1) < GMLP_GROUP_DIM
    bs = bs_ref[...]
    for c in range(tm // GMLP_CHUNK):
        r0 = c * GMLP_CHUNK
        pieces = []
        for p in range(groups // 2):
            vgp = vg[r0:r0 + GMLP_CHUNK, p * 2 * GMLP_GROUP_DIM:(p + 1) * 2 * GMLP_GROUP_DIM]
            pieces.append(jnp.where(first_half, _mm(w[2 * p], vgp), _mm(w[2 * p + 1], vgp)))
        mixed = jnp.concatenate(pieces, axis=1) + bs
        a_ref[r0:r0 + GMLP_CHUNK, :] = (u[r0:r0 + GMLP_CHUNK] * mixed).astype(BF16)


def _even_in(x, g, wqkv, wf, fb, wz, lng, lnb, ws, bs, *, heads):
    B, T, D = x.shape
    tm = TOKEN_TILE
    gw = wz.shape[1] // 2
    tile = lambda w_: pl.BlockSpec((None, tm, w_), lambda b, t: (b, t, 0))
    head_tile = pl.BlockSpec((None, heads, tm, FOX_HEAD_DIM), lambda b, t: (b, 0, t, 0))
    head_shape = jax.ShapeDtypeStruct((B, heads, T, FOX_HEAD_DIM), BF16)
    return pl.pallas_call(
        functools.partial(_even_in_kernel, heads=heads),
        grid=(B, T // tm),
        in_specs=[tile(D), _full(g), _full(wqkv), _full(wf), _full(fb), _full(wz), _full(lng),
                  _full(lnb), _full(ws), _full(bs)],
        out_specs=[head_tile, head_tile, head_tile, tile(LANES_V7X), tile(gw)],
        out_shape=[head_shape, head_shape, head_shape,
                   jax.ShapeDtypeStruct((B, T, LANES_V7X), F32),
                   jax.ShapeDtypeStruct((B, T, gw), BF16)],
        compiler_params=_params(2),
        name="even_in",
    )(x, g, wqkv, wf, fb, wz, lng, lnb, ws, bs)


def _fox_cum_kernel(lf_ref, ch_ref, *, heads):
    x = lf_ref[...]
    T = x.shape[0]
    row = lax.broadcasted_iota(jnp.int32, x.shape, 0)
    shift = 1
    while shift < T:
        x = x + jnp.where(row >= shift, pltpu.roll(x, shift, axis=0), 0.0)
        shift *= 2
    ch_ref[...] = x.T[:heads]


def _fox_cum(lf, *, heads):
    B, T, L = lf.shape
    return pl.pallas_call(
        functools.partial(_fox_cum_kernel, heads=heads),
        grid=(B,),
        in_specs=[pl.BlockSpec((None, T, L), lambda b: (b, 0, 0))],
        out_specs=pl.BlockSpec((None, heads, T), lambda b: (b, 0, 0)),
        out_shape=jax.ShapeDtypeStruct((B, heads, T), F32),
        compiler_params=_params(1),
        name="fox_cum",
    )(lf)


def _fox_attn_kernel(q_ref, k_ref, v_ref, ch_ref, o_ref):
    T = q_ref.shape[1]
    tq = FOX_BLOCK
    row = lax.broadcasted_iota(jnp.int32, (tq, tq), 0)
    col = lax.broadcasted_iota(jnp.int32, (tq, tq), 1)
    causal = row >= col
    nt = (((1,), (1,)), ((), ()))

    for i in range(T // tq):
        r0 = i * tq
        outs = []
        for j in range(2):
            q = q_ref[j, r0:r0 + tq, :]
            s_d = lax.dot_general(q, k_ref[j, r0:r0 + tq, :], nt, preferred_element_type=F32)
            s_d = jnp.where(causal, s_d - ch_ref[j:j + 1, r0:r0 + tq], -jnp.inf)
            m = jnp.max(s_d, axis=1, keepdims=True)
            if i > 0:
                s_o = lax.dot_general(q, k_ref[j, :r0, :], nt, preferred_element_type=F32)
                s_o = s_o - ch_ref[j:j + 1, :r0]
                m = jnp.maximum(m, jnp.max(s_o, axis=1, keepdims=True))
            p_d = jnp.exp(s_d - m)
            l = jnp.sum(p_d, axis=1, keepdims=True)
            acc = _mm(p_d.astype(BF16), v_ref[j, r0:r0 + tq, :])
            if i > 0:
                p_o = jnp.exp(s_o - m)
                l = l + jnp.sum(p_o, axis=1, keepdims=True)
                acc = acc + _mm(p_o.astype(BF16), v_ref[j, :r0, :])
            outs.append(acc * (1.0 / l))
        o_ref[r0:r0 + tq, :] = jnp.concatenate(outs, axis=1).astype(BF16)


def _fox_attn(q, k, v, ch):
    B, H, T, hd = q.shape
    ch4 = ch.reshape(B, H // 2, 2, T)
    qkv_spec = pl.BlockSpec((None, 2, T, hd), lambda b, h: (b, h, 0, 0))
    return pl.pallas_call(
        _fox_attn_kernel,
        grid=(B, H // 2),
        in_specs=[qkv_spec, qkv_spec, qkv_spec,
                  pl.BlockSpec((None, None, 2, T), lambda b, h: (b, h, 0, 0))],
        out_specs=pl.BlockSpec((None, T, 2 * hd), lambda b, h: (b, 0, h)),
        out_shape=jax.ShapeDtypeStruct((B, T, H * hd), BF16),
        compiler_params=_params(2),
        name="fox_attn",
    )(q, k, v, ch4)


def _mem_kv_kernel(m_ref, g_ref, wkv_ref, kt_ref, v_ref):
    m = _rmsnorm(m_ref[...], g_ref[...]).astype(BF16)
    kv = _mm(m, wkv_ref[...])
    d = kv.shape[1] // 2
    kt_ref[...] = kv[:, :d].T.astype(BF16)
    v_ref[...] = kv[:, d:].astype(BF16)


def _mem_kv(mem, g, wkv):
    B, M, D = mem.shape
    depth = wkv.shape[0]
    return pl.pallas_call(
        _mem_kv_kernel,
        grid=(depth, B),
        in_specs=[pl.BlockSpec((None, M, D), lambda l, b: (b, 0, 0)),
                  pl.BlockSpec((None, 1, D), lambda l, b: (l, 0, 0)),
                  pl.BlockSpec((None, D, 2 * D), lambda l, b: (l, 0, 0))],
        out_specs=[pl.BlockSpec((None, None, D, M), lambda l, b: (l, b, 0, 0)),
                   pl.BlockSpec((None, None, M, D), lambda l, b: (l, b, 0, 0))],
        out_shape=[jax.ShapeDtypeStruct((depth, B, D, M), BF16),
                   jax.ShapeDtypeStruct((depth, B, M, D), BF16)],
        compiler_params=_params(2),
        name="mem_kv",
    )(mem, g, wkv)


def _xattn_body(x, g_ref, wq_ref, kt_ref, v_ref, wo_ref):
    hd = wq_ref.shape[1] // XA_HEADS
    h = _rmsnorm(x, g_ref[...]).astype(BF16)
    q = (_mm(h, wq_ref[...]) * hd ** -0.5).astype(BF16)
    outs = []
    for hh in range(XA_HEADS):
        lo = hh * hd
        s = _mm(q[:, lo:lo + hd], kt_ref[lo:lo + hd, :])
        p = jnp.exp(s - jnp.max(s, axis=1, keepdims=True))
        p = (p / jnp.sum(p, axis=1, keepdims=True)).astype(BF16)
        outs.append(_mm(p, v_ref[:, lo:lo + hd]).astype(BF16))
    return x + _mm(jnp.concatenate(outs, axis=1), wo_ref[...])


def _xattn_kernel(x_ref, g_ref, wq_ref, kt_ref, v_ref, wo_ref, o_ref):
    o_ref[...] = _xattn_body(x_ref[...], g_ref, wq_ref, kt_ref, v_ref, wo_ref)


def _mix_xattn_kernel(x_ref, b_ref, a_ref, wout_ref, g_ref, wq_ref, kt_ref, v_ref, wo_ref, o_ref):
    fw = b_ref.shape[1]
    x = x_ref[...] + _mm(b_ref[...], wout_ref[:fw, :]) + _mm(a_ref[...], wout_ref[fw:, :])
    o_ref[...] = _xattn_body(x, g_ref, wq_ref, kt_ref, v_ref, wo_ref)


def _xattn(x, mix, g, wq, kt, v, wo):
    B, T, D = x.shape
    tm = TOKEN_TILE
    M = v.shape[1]
    tile = lambda w_: pl.BlockSpec((None, tm, w_), lambda b, t: (b, t, 0))
    kv_specs = [pl.BlockSpec((None, D, M), lambda b, t: (b, 0, 0)),
                pl.BlockSpec((None, M, D), lambda b, t: (b, 0, 0))]
    if mix is None:
        kern, pre_args, pre_specs = _xattn_kernel, (), []
    else:
        b_out, a_out, w_out = mix
        kern, pre_args = _mix_xattn_kernel, (b_out, a_out, w_out)
        pre_specs = [tile(b_out.shape[2]), tile(a_out.shape[2]), _full(w_out)]
    return pl.pallas_call(
        kern,
        grid=(B, T // tm),
        in_specs=[tile(D)] + pre_specs + [_full(g), _full(wq)] + kv_specs + [_full(wo)],
        out_specs=tile(D),
        out_shape=jax.ShapeDtypeStruct((B, T, D), F32),
        compiler_params=_params(2),
        name="xattn" if mix is None else "mix_xattn",
    )(x, *pre_args, g, wq, kt, v, wo)


def _ffn_kernel(x_ref, g_ref, wg_ref, wu_ref, wd_ref, fg_ref, o_ref, *, final):
    x = x_ref[...]
    h = _rmsnorm(x, g_ref[...]).astype(BF16)
    hc = wg_ref.shape[1] // FFN_CHUNKS
    y = x
    for c in range(FFN_CHUNKS):
        lo = c * hc
        a = jax.nn.silu(_mm(h, wg_ref[:, lo:lo + hc])) * _mm(h, wu_ref[:, lo:lo + hc])
        y = y + _mm(a.astype(BF16), wd_ref[lo:lo + hc, :])
    if final:
        y = _rmsnorm(y, fg_ref[...])
    o_ref[...] = y


def _ffn(x, g, wg, wu, wd, fg, *, final):
    B, T, D = x.shape
    tm = TOKEN_TILE
    tile = pl.BlockSpec((None, tm, D), lambda b, t: (b, t, 0))
    return pl.pallas_call(
        functools.partial(_ffn_kernel, final=final),
        grid=(B, T // tm),
        in_specs=[tile, _full(g), _full(wg), _full(wu), _full(wd), _full(fg)],
        out_specs=tile,
        out_shape=jax.ShapeDtypeStruct((B, T, D), F32),
        compiler_params=_params(2),
        name="ffn_final" if final else "ffn",
    )(x, g, wg, wu, wd, fg)


def _conv_kernel(x_ref, g_ref, win_ref, bin_ref, dww_ref, dwb_ref, lng_ref, lnb_ref, wout_ref,
                 bout_ref, o_ref, y_buf):
    x = x_ref[...]
    tm = x.shape[0]
    taps = dww_ref.shape[0]

    @pl.when(pl.program_id(1) == 0)
    def _():
        y_buf[:CONV_HALO, :] = jnp.zeros((CONV_HALO, y_buf.shape[1]), F32)

    h = _rmsnorm(x, g_ref[...]).astype(BF16)
    ag = _mm(h, win_ref[...]) + bin_ref[...]
    cw = ag.shape[1] // 2
    y_buf[CONV_HALO:, :] = ag[:, :cw] * jax.nn.sigmoid(ag[:, cw:])

    base = CONV_HALO - (taps - 1)
    acc = jnp.zeros((tm, cw), F32)
    for j in range(taps):
        acc = acc + dww_ref[j:j + 1, :] * y_buf[base + j:base + j + tm, :]
    y_buf[:CONV_HALO, :] = y_buf[tm:tm + CONV_HALO, :]

    y = jax.nn.silu(_layernorm(acc + dwb_ref[...], lng_ref[...], lnb_ref[...]))
    o_ref[...] = x + _mm(y.astype(BF16), wout_ref[...]) + bout_ref[...]


def _conv(x, g, win, bin_, dww, dwb, lng, lnb, wout, bout):
    B, T, D = x.shape
    tm = TOKEN_TILE
    cw = wout.shape[0]
    tile = pl.BlockSpec((None, tm, D), lambda b, t: (b, t, 0))
    return pl.pallas_call(
        _conv_kernel,
        grid=(B, T // tm),
        in_specs=[tile, _full(g), _full(win), _full(bin_), _full(dww), _full(dwb), _full(lng),
                  _full(lnb), _full(wout), _full(bout)],
        out_specs=tile,
        out_shape=jax.ShapeDtypeStruct((B, T, D), F32),
        scratch_shapes=[pltpu.VMEM((CONV_HALO + tm, cw), F32)],
        compiler_params=_params(2),
        name="conv",
    )(x, g, win, bin_, dww, dwb, lng, lnb, wout, bout)


def kernel(x, mem, mix_norm_e, w_in_e, fox_f_bias, gmlp_ln_g, gmlp_ln_b, gmlp_w_s, gmlp_b_s, w_out_e, mix_norm_o, conv_w_in, conv_b_in, conv_dw_w, conv_dw_b, conv_ln_g, conv_ln_b, conv_w_out, conv_b_out, xa_norm, mem_norm, xa_wq, xa_wkv, xa_wo, ffn_norm, ffn_w_gu, ffn_w_down, final_norm):
    D = x.shape[2]
    depth = xa_wq.shape[0]
    heads = fox_f_bias.shape[1]
    fw = heads * FOX_HEAD_DIM
    groups = gmlp_w_s.shape[1]
    gw = groups * GMLP_GROUP_DIM
    hidden = ffn_w_down.shape[1]
    assert 2 * GMLP_GROUP_DIM == LANES_V7X and groups % 2 == 0 and heads % 2 == 0
    assert w_in_e.shape[2] == 3 * fw + heads + 2 * gw and conv_dw_w.shape[1] <= CONV_HALO + 1

    row = lambda p: p.reshape(1, -1)
    kt, vm = _mem_kv(mem, mem_norm.reshape(depth, 1, D), xa_wkv.astype(BF16))

    for layer in range(depth):
        li = layer // 2
        if layer % 2 == 0:
            w_in = w_in_e[li]
            wf = jnp.pad(w_in[:, 3 * fw:3 * fw + heads], ((0, 0), (0, LANES_V7X - heads)))
            fb = jnp.pad(fox_f_bias[li], (0, LANES_V7X - heads)).reshape(1, -1)
            bs = jnp.repeat(gmlp_b_s[li].T, GMLP_GROUP_DIM, axis=1)
            q, k, v, lf, a_out = _even_in(
                x, row(mix_norm_e[li]), w_in[:, :3 * fw].astype(BF16), wf.astype(BF16), fb,
                w_in[:, 3 * fw + heads:].astype(BF16), row(gmlp_ln_g[li]), row(gmlp_ln_b[li]),
                gmlp_w_s[li], bs, heads=heads)
            b_out = _fox_attn(q, k, v, _fox_cum(lf, heads=heads))
            mix = (b_out, a_out, w_out_e[li].astype(BF16))
        else:
            x = _conv(x, row(mix_norm_o[li]), conv_w_in[li].astype(BF16), row(conv_b_in[li]),
                      conv_dw_w[li], row(conv_dw_b[li]), row(conv_ln_g[li]), row(conv_ln_b[li]),
                      conv_w_out[li].astype(BF16), row(conv_b_out[li]))
            mix = None
        x = _xattn(x, mix, row(xa_norm[layer]), xa_wq[layer].astype(BF16), kt[layer], vm[layer],
                   xa_wo[layer].astype(BF16))
        w_gu = ffn_w_gu[layer].astype(BF16)
        x = _ffn(x, row(ffn_norm[layer]), w_gu[:, :hidden], w_gu[:, hidden:],
                 ffn_w_down[layer].astype(BF16), row(final_norm), final=layer == depth - 1)
    return x
```

```python
import functools

import jax
import jax.numpy as jnp
from jax import lax
from jax.experimental import pallas as pl
from jax.experimental.pallas import tpu as pltpu

F32 = jnp.float32
BF16 = jnp.bfloat16
EPS = 1e-6
LOG2_E = 1.4426950408889634
NT_DIMS = (((1,), (1,)), ((), ()))

LANES_V7X = 128
SUBLANES_V7X = 8
MXU_TILE_V7X = 256
VMEM_LIMIT_V7X = 56 * 1024 * 1024

FOX_HEAD_DIM = 64
GMLP_GROUP_DIM = 64
GMLP_CHUNK = 128
XA_HEADS = 4
CONV_HALO = 32

TOKEN_TILE = 512
FFN_TOKEN_TILE = 1024
FOX_BLOCK = 256
FFN_CHUNKS = 2


def _rmsnorm(x, g):
    return x * lax.rsqrt(jnp.mean(x * x, axis=-1, keepdims=True) + EPS) * g


def _layernorm(x, g, b):
    mu = jnp.mean(x, axis=-1, keepdims=True)
    xc = x - mu
    return xc * lax.rsqrt(jnp.mean(xc * xc, axis=-1, keepdims=True) + EPS) * g + b


def _mm(a, b):
    return jnp.dot(a, b, preferred_element_type=F32)


def _full(a):
    nd = a.ndim
    return pl.BlockSpec(a.shape, lambda *_: (0,) * nd, pipeline_mode=pl.Buffered(1))


def _params(n_grid):
    return pltpu.CompilerParams(
        dimension_semantics=("arbitrary",) * n_grid, vmem_limit_bytes=VMEM_LIMIT_V7X)


def _even_in_kernel(x_ref, g_ref, wqkv_ref, wf_ref, fb_ref, wz_ref, lng_ref, lnb_ref,
                    ws_ref, bs_ref, q_ref, k_ref, v_ref, lf_ref, a_ref, *, heads):
    x = x_ref[...]
    tm = x.shape[0]
    h = _rmsnorm(x, g_ref[...]).astype(BF16)

    qkv = _mm(h, wqkv_ref[...])
    width = heads * FOX_HEAD_DIM
    scale = FOX_HEAD_DIM ** -0.5 * LOG2_E
    for hd in range(heads):
        lo = hd * FOX_HEAD_DIM
        q_ref[hd] = (qkv[:, lo:lo + FOX_HEAD_DIM] * scale).astype(BF16)
        k_ref[hd] = qkv[:, width + lo:width + lo + FOX_HEAD_DIM].astype(BF16)
        v_ref[hd] = qkv[:, 2 * width + lo:2 * width + lo + FOX_HEAD_DIM].astype(BF16)

    lf_ref[...] = jax.nn.log_sigmoid(_mm(h, wf_ref[...]) + fb_ref[...])

    z = jax.nn.gelu(_mm(h, wz_ref[...]))
    gw = z.shape[1] // 2
    u = z[:, :gw]
    vg = _layernorm(z[:, gw:], lng_ref[...], lnb_ref[...]).astype(BF16)

    groups = gw // GMLP_GROUP_DIM
    row = lax.broadcasted_iota(jnp.int32, (GMLP_CHUNK, GMLP_CHUNK), 0)
    col = lax.broadcasted_iota(jnp.int32, (GMLP_CHUNK, GMLP_CHUNK), 1)
    causal = row >= col
    w = [jnp.where(causal, ws_ref[g], 0.0).astype(BF16) for g in range(groups)]
    pair = 2 * GMLP_GROUP_DIM
    lane = lax.broadcasted_iota(jnp.int32, (GMLP_CHUNK, 2 * pair), 1)
    first_group = (lane % pair) < GMLP_GROUP_DIM
    bs = bs_ref[...]
    for c in range(tm // (2 * GMLP_CHUNK)):
        r0 = 2 * c * GMLP_CHUNK
        r1 = r0 + GMLP_CHUNK
        pieces = ([], [])
        for p in range(groups // 2):
            vgp = jnp.concatenate([vg[r0:r1, p * pair:(p + 1) * pair],
                                   vg[r1:r1 + GMLP_CHUNK, p * pair:(p + 1) * pair]], axis=1)
            mix = jnp.where(first_group, _mm(w[2 * p], vgp), _mm(w[2 * p + 1], vgp))
            pieces[0].append(mix[:, :pair])
            pieces[1].append(mix[:, pair:])
        for e, lo in enumerate((r0, r1)):
            mixed = jnp.concatenate(pieces[e], axis=1) + bs
            a_ref[lo:lo + GMLP_CHUNK, :] = (u[lo:lo + GMLP_CHUNK] * mixed).astype(BF16)


def _even_in(x, g, wqkv, wf, fb, wz, lng, lnb, ws, bs, *, heads):
    B, T, D = x.shape
    tm = TOKEN_TILE
    gw = wz.shape[1] // 2
    tile = lambda w_: pl.BlockSpec((None, tm, w_), lambda b, t: (b, t, 0))
    head_tile = pl.BlockSpec((None, heads, tm, FOX_HEAD_DIM), lambda b, t: (b, 0, t, 0))
    head_shape = jax.ShapeDtypeStruct((B, heads, T, FOX_HEAD_DIM), BF16)
    return pl.pallas_call(
        functools.partial(_even_in_kernel, heads=heads),
        grid=(B, T // tm),
        in_specs=[tile(D), _full(g), _full(wqkv), _full(wf), _full(fb), _full(wz), _full(lng),
                  _full(lnb), _full(ws), _full(bs)],
        out_specs=[head_tile, head_tile, head_tile, tile(LANES_V7X), tile(gw)],
        out_shape=[head_shape, head_shape, head_shape,
                   jax.ShapeDtypeStruct((B, T, LANES_V7X), F32),
                   jax.ShapeDtypeStruct((B, T, gw), BF16)],
        compiler_params=_params(2),
        name="even_in",
    )(x, g, wqkv, wf, fb, wz, lng, lnb, ws, bs)


def _fox_cum_kernel(lf_ref, ch_ref, *, heads):
    x = lf_ref[...]
    T = x.shape[0]
    row = lax.broadcasted_iota(jnp.int32, x.shape, 0)
    shift = 1
    while shift < T:
        x = x + jnp.where(row >= shift, pltpu.roll(x, shift, axis=0), 0.0)
        shift *= 2
    ch_ref[...] = x.T[:heads] * LOG2_E


def _fox_cum(lf, *, heads):
    B, T, L = lf.shape
    return pl.pallas_call(
        functools.partial(_fox_cum_kernel, heads=heads),
        grid=(B,),
        in_specs=[pl.BlockSpec((None, T, L), lambda b: (b, 0, 0))],
        out_specs=pl.BlockSpec((None, heads, T), lambda b: (b, 0, 0)),
        out_shape=jax.ShapeDtypeStruct((B, heads, T), F32),
        compiler_params=_params(1),
        name="fox_cum",
    )(lf)


def _fox_attn_kernel(q_ref, k_ref, v_ref, ch_ref, o_ref):
    T = q_ref.shape[1]
    tq = FOX_BLOCK
    row = lax.broadcasted_iota(jnp.int32, (tq, tq), 0)
    col = lax.broadcasted_iota(jnp.int32, (tq, tq), 1)
    causal = row >= col

    for i in range(T // tq):
        r0 = i * tq
        outs = []
        for j in range(2):
            q = q_ref[j, r0:r0 + tq, :]
            s_d = lax.dot_general(q, k_ref[j, r0:r0 + tq, :], NT_DIMS, preferred_element_type=F32)
            s_d = jnp.where(causal, s_d - ch_ref[j:j + 1, r0:r0 + tq], -jnp.inf)
            m = jnp.max(s_d, axis=1, keepdims=True)
            if i > 0:
                s_o = lax.dot_general(q, k_ref[j, :r0, :], NT_DIMS, preferred_element_type=F32)
                s_o = s_o - ch_ref[j:j + 1, :r0]
                m = jnp.maximum(m, jnp.max(s_o, axis=1, keepdims=True))
            p_d = jnp.exp2(s_d - m)
            l = jnp.sum(p_d, axis=1, keepdims=True)
            acc = _mm(p_d.astype(BF16), v_ref[j, r0:r0 + tq, :])
            if i > 0:
                p_o = jnp.exp2(s_o - m)
                l = l + jnp.sum(p_o, axis=1, keepdims=True)
                acc = acc + _mm(p_o.astype(BF16), v_ref[j, :r0, :])
            outs.append(acc * (1.0 / l))
        o_ref[r0:r0 + tq, :] = jnp.concatenate(outs, axis=1).astype(BF16)


def _fox_attn(q, k, v, ch):
    B, H, T, hd = q.shape
    ch4 = ch.reshape(B, H // 2, 2, T)
    qkv_spec = pl.BlockSpec((None, 2, T, hd), lambda b, h: (b, h, 0, 0))
    return pl.pallas_call(
        _fox_attn_kernel,
        grid=(B, H // 2),
        in_specs=[qkv_spec, qkv_spec, qkv_spec,
                  pl.BlockSpec((None, None, 2, T), lambda b, h: (b, h, 0, 0))],
        out_specs=pl.BlockSpec((None, T, 2 * hd), lambda b, h: (b, 0, h)),
        out_shape=jax.ShapeDtypeStruct((B, T, H * hd), BF16),
        compiler_params=_params(2),
        name="fox_attn",
    )(q, k, v, ch4)


def _mem_kv_kernel(m_ref, g_ref, wkv_ref, kt_ref, v_ref):
    m = _rmsnorm(m_ref[...], g_ref[...]).astype(BF16)
    kv = _mm(m, wkv_ref[...])
    d = kv.shape[1] // 2
    kt_ref[...] = kv[:, :d].T.astype(BF16)
    v_ref[...] = kv[:, d:].astype(BF16)


def _mem_kv(mem, g, wkv):
    B, M, D = mem.shape
    depth = wkv.shape[0]
    return pl.pallas_call(
        _mem_kv_kernel,
        grid=(depth, B),
        in_specs=[pl.BlockSpec((None, M, D), lambda l, b: (b, 0, 0)),
                  pl.BlockSpec((None, 1, D), lambda l, b: (l, 0, 0)),
                  pl.BlockSpec((None, D, 2 * D), lambda l, b: (l, 0, 0))],
        out_specs=[pl.BlockSpec((None, None, D, M), lambda l, b: (l, b, 0, 0)),
                   pl.BlockSpec((None, None, M, D), lambda l, b: (l, b, 0, 0))],
        out_shape=[jax.ShapeDtypeStruct((depth, B, D, M), BF16),
                   jax.ShapeDtypeStruct((depth, B, M, D), BF16)],
        compiler_params=_params(2),
        name="mem_kv",
    )(mem, g, wkv)


def _xattn_body(x, g_ref, wq_ref, kt_ref, v_ref, wo_ref):
    hd = wq_ref.shape[1] // XA_HEADS
    h = _rmsnorm(x, g_ref[...]).astype(BF16)
    q = (_mm(h, wq_ref[...]) * hd ** -0.5).astype(BF16)
    outs = []
    for hh in range(XA_HEADS):
        lo = hh * hd
        s = _mm(q[:, lo:lo + hd], kt_ref[lo:lo + hd, :])
        p = jnp.exp(s - jnp.max(s, axis=1, keepdims=True))
        p = (p / jnp.sum(p, axis=1, keepdims=True)).astype(BF16)
        outs.append(_mm(p, v_ref[:, lo:lo + hd]).astype(BF16))
    return x + _mm(jnp.concatenate(outs, axis=1), wo_ref[...])


def _xattn_kernel(x_ref, g_ref, wq_ref, kt_ref, v_ref, wo_ref, o_ref):
    o_ref[...] = _xattn_body(x_ref[...], g_ref, wq_ref, kt_ref, v_ref, wo_ref)


def _mix_xattn_kernel(x_ref, b_ref, a_ref, wout_ref, g_ref, wq_ref, kt_ref, v_ref, wo_ref, o_ref):
    fw = b_ref.shape[1]
    x = x_ref[...] + _mm(b_ref[...], wout_ref[:fw, :]) + _mm(a_ref[...], wout_ref[fw:, :])
    o_ref[...] = _xattn_body(x, g_ref, wq_ref, kt_ref, v_ref, wo_ref)


def _xattn(x, mix, g, wq, kt, v, wo):
    B, T, D = x.shape
    tm = TOKEN_TILE
    M = v.shape[1]
    tile = lambda w_: pl.BlockSpec((None, tm, w_), lambda b, t: (b, t, 0))
    kv_specs = [pl.BlockSpec((None, D, M), lambda b, t: (b, 0, 0)),
                pl.BlockSpec((None, M, D), lambda b, t: (b, 0, 0))]
    if mix is None:
        kern, pre_args, pre_specs = _xattn_kernel, (), []
    else:
        b_out, a_out, w_out = mix
        kern, pre_args = _mix_xattn_kernel, (b_out, a_out, w_out)
        pre_specs = [tile(b_out.shape[2]), tile(a_out.shape[2]), _full(w_out)]
    return pl.pallas_call(
        kern,
        grid=(B, T // tm),
        in_specs=[tile(D)] + pre_specs + [_full(g), _full(wq)] + kv_specs + [_full(wo)],
        out_specs=tile(D),
        out_shape=jax.ShapeDtypeStruct((B, T, D), F32),
        compiler_params=_params(2),
        name="xattn" if mix is None else "mix_xattn",
    )(x, *pre_args, g, wq, kt, v, wo)


def _ffn_kernel(x_ref, g_ref, wg_ref, wu_ref, wd_ref, fg_ref, o_ref, *, final):
    x = x_ref[...]
    h = _rmsnorm(x, g_ref[...]).astype(BF16)
    hidden = wg_ref.shape[1]
    hc = pl.cdiv(pl.cdiv(hidden, FFN_CHUNKS), MXU_TILE_V7X) * MXU_TILE_V7X
    y = x
    for lo in range(0, hidden, hc):
        hi = min(lo + hc, hidden)
        a = jax.nn.silu(_mm(h, wg_ref[:, lo:hi])) * _mm(h, wu_ref[:, lo:hi])
        y = y + _mm(a.astype(BF16), wd_ref[lo:hi, :])
    if final:
        y = _rmsnorm(y, fg_ref[...])
    o_ref[...] = y


def _ffn(x, g, wg, wu, wd, fg, *, final):
    B, T, D = x.shape
    tm = FFN_TOKEN_TILE
    tile = pl.BlockSpec((None, tm, D), lambda b, t: (b, t, 0))
    return pl.pallas_call(
        functools.partial(_ffn_kernel, final=final),
        grid=(B, T // tm),
        in_specs=[tile, _full(g), _full(wg), _full(wu), _full(wd), _full(fg)],
        out_specs=tile,
        out_shape=jax.ShapeDtypeStruct((B, T, D), F32),
        compiler_params=_params(2),
        name="ffn_final" if final else "ffn",
    )(x, g, wg, wu, wd, fg)


def _conv_kernel(x_ref, g_ref, win_ref, bin_ref, dww_ref, dwb_ref, lng_ref, lnb_ref, wout_ref,
                 bout_ref, o_ref, y_buf):
    x = x_ref[...]
    tm = x.shape[0]
    taps = dww_ref.shape[0]

    @pl.when(pl.program_id(1) == 0)
    def _():
        y_buf[:CONV_HALO, :] = jnp.zeros((CONV_HALO, y_buf.shape[1]), F32)

    h = _rmsnorm(x, g_ref[...]).astype(BF16)
    ag = _mm(h, win_ref[...]) + bin_ref[...]
    cw = ag.shape[1] // 2
    y_buf[CONV_HALO:, :] = ag[:, :cw] * jax.nn.sigmoid(ag[:, cw:])

    first = CONV_HALO - (taps - 1)
    n = CONV_HALO + tm
    yb = y_buf[...]
    acc = jnp.zeros((tm, cw), F32)
    for k in range(SUBLANES_V7X):
        shifted = yb if k == 0 else pltpu.roll(yb, n - k, axis=0)
        for a in range(CONV_HALO // SUBLANES_V7X + 1):
            j = SUBLANES_V7X * a + k - first
            if 0 <= j < taps:
                r0 = SUBLANES_V7X * a
                acc = acc + dww_ref[j:j + 1, :] * shifted[r0:r0 + tm, :]
    y_buf[:CONV_HALO, :] = y_buf[tm:tm + CONV_HALO, :]

    y = jax.nn.silu(_layernorm(acc + dwb_ref[...], lng_ref[...], lnb_ref[...]))
    o_ref[...] = x + _mm(y.astype(BF16), wout_ref[...]) + bout_ref[...]


def _conv(x, g, win, bin_, dww, dwb, lng, lnb, wout, bout):
    B, T, D = x.shape
    tm = TOKEN_TILE
    cw = wout.shape[0]
    tile = pl.BlockSpec((None, tm, D), lambda b, t: (b, t, 0))
    return pl.pallas_call(
        _conv_kernel,
        grid=(B, T // tm),
        in_specs=[tile, _full(g), _full(win), _full(bin_), _full(dww), _full(dwb), _full(lng),
                  _full(lnb), _full(wout), _full(bout)],
        out_specs=tile,
        out_shape=jax.ShapeDtypeStruct((B, T, D), F32),
        scratch_shapes=[pltpu.VMEM((CONV_HALO + tm, cw), F32)],
        compiler_params=_params(2),
        name="conv",
    )(x, g, win, bin_, dww, dwb, lng, lnb, wout, bout)


def kernel(x, mem, mix_norm_e, w_in_e, fox_f_bias, gmlp_ln_g, gmlp_ln_b, gmlp_w_s, gmlp_b_s, w_out_e, mix_norm_o, conv_w_in, conv_b_in, conv_dw_w, conv_dw_b, conv_ln_g, conv_ln_b, conv_w_out, conv_b_out, xa_norm, mem_norm, xa_wq, xa_wkv, xa_wo, ffn_norm, ffn_w_gu, ffn_w_down, final_norm):
    D = x.shape[2]
    depth = xa_wq.shape[0]
    heads = fox_f_bias.shape[1]
    fw = heads * FOX_HEAD_DIM
    groups = gmlp_w_s.shape[1]
    gw = groups * GMLP_GROUP_DIM
    hidden = ffn_w_down.shape[1]
    assert 2 * GMLP_GROUP_DIM == LANES_V7X and groups % 2 == 0 and heads % 2 == 0
    assert w_in_e.shape[2] == 3 * fw + heads + 2 * gw and conv_dw_w.shape[1] <= CONV_HALO + 1

    row = lambda p: p.reshape(1, -1)
    kt, vm = _mem_kv(mem, mem_norm.reshape(depth, 1, D), xa_wkv.astype(BF16))

    for layer in range(depth):
        li = layer // 2
        if layer % 2 == 0:
            w_in = w_in_e[li]
            wf = jnp.pad(w_in[:, 3 * fw:3 * fw + heads], ((0, 0), (0, LANES_V7X - heads)))
            fb = jnp.pad(fox_f_bias[li], (0, LANES_V7X - heads)).reshape(1, -1)
            bs = jnp.repeat(gmlp_b_s[li].T, GMLP_GROUP_DIM, axis=1)
            q, k, v, lf, a_out = _even_in(
                x, row(mix_norm_e[li]), w_in[:, :3 * fw].astype(BF16), wf.astype(BF16), fb,
                w_in[:, 3 * fw + heads:].astype(BF16), row(gmlp_ln_g[li]), row(gmlp_ln_b[li]),
                gmlp_w_s[li], bs, heads=heads)
            b_out = _fox_attn(q, k, v, _fox_cum(lf, heads=heads))
            mix = (b_out, a_out, w_out_e[li].astype(BF16))
        else:
            x = _conv(x, row(mix_norm_o[li]), conv_w_in[li].astype(BF16), row(conv_b_in[li]),
                      conv_dw_w[li], row(conv_dw_b[li]), row(conv_ln_g[li]), row(conv_ln_b[li]),
                      conv_w_out[li].astype(BF16), row(conv_b_out[li]))
            mix = None
        x = _xattn(x, mix, row(xa_norm[layer]), xa_wq[layer].astype(BF16), kt[layer], vm[layer],
                   xa_wo[layer].astype(BF16))
        w_gu = ffn_w_gu[layer].astype(BF16)
        x = _ffn(x, row(ffn_norm[layer]), w_gu[:, :hidden], w_gu[:, hidden:],
                 ffn_w_down[layer].astype(BF16), row(final_norm), final=layer == depth - 1)
    return x
```

```python
import functools

import jax
import jax.numpy as jnp
from jax import lax
from jax.experimental import pallas as pl
from jax.experimental.pallas import tpu as pltpu

F32 = jnp.float32
BF16 = jnp.bfloat16
EPS = 1e-6
LOG2_E = 1.4426950408889634
NT_DIMS = (((1,), (1,)), ((), ()))

LANES_V7X = 128
SUBLANES_V7X = 8
MXU_TILE_V7X = 256
VMEM_LIMIT_V7X = 56 * 1024 * 1024

FOX_HEAD_DIM = 64
GMLP_GROUP_DIM = 64
GMLP_CHUNK = 128
XA_HEADS = 4
CONV_HALO = 32

TOKEN_TILE = 512
FFN_TOKEN_TILE = 1024
FOX_BLOCK = 256
FFN_CHUNKS = 2


def _rmsnorm(x, g):
    return x * lax.rsqrt(jnp.mean(x * x, axis=-1, keepdims=True) + EPS) * g


def _layernorm(x, g, b):
    mu = jnp.mean(x, axis=-1, keepdims=True)
    xc = x - mu
    return xc * lax.rsqrt(jnp.mean(xc * xc, axis=-1, keepdims=True) + EPS) * g + b


def _mm(a, b):
    return jnp.dot(a, b, preferred_element_type=F32)


def _full(a):
    nd = a.ndim
    return pl.BlockSpec(a.shape, lambda *_: (0,) * nd, pipeline_mode=pl.Buffered(1))


def _params(n_grid):
    return pltpu.CompilerParams(
        dimension_semantics=("arbitrary",) * n_grid, vmem_limit_bytes=VMEM_LIMIT_V7X)


def _even_in_kernel(x_ref, g_ref, wqt_ref, wk_ref, wvt_ref, wf_ref, fb_ref, wz_ref, lng_ref, lnb_ref,
                    ws_ref, bs_ref, qt_ref, k_ref, vt_ref, lf_ref, a_ref):
    x = x_ref[...]
    tm = x.shape[0]
    h = _rmsnorm(x, g_ref[...]).astype(BF16)

    z = jax.nn.gelu(_mm(h, wz_ref[...]))
    gw = z.shape[1] // 2
    u = z[:, :gw]
    vg = _layernorm(z[:, gw:], lng_ref[...], lnb_ref[...]).astype(BF16)

    scale = FOX_HEAD_DIM ** -0.5 * LOG2_E
    qt_ref[...] = (lax.dot_general(wqt_ref[...], h, NT_DIMS, preferred_element_type=F32)
                   * scale).astype(BF16)
    k_ref[...] = _mm(h, wk_ref[...]).astype(BF16)
    vt_ref[...] = lax.dot_general(wvt_ref[...], h, NT_DIMS,
                                  preferred_element_type=F32).astype(BF16)

    lf_ref[...] = jax.nn.log_sigmoid(_mm(h, wf_ref[...]) + fb_ref[...])

    groups = gw // GMLP_GROUP_DIM
    row = lax.broadcasted_iota(jnp.int32, (GMLP_CHUNK, GMLP_CHUNK), 0)
    col = lax.broadcasted_iota(jnp.int32, (GMLP_CHUNK, GMLP_CHUNK), 1)
    causal = row >= col
    w = [jnp.where(causal, ws_ref[g], 0.0).astype(BF16) for g in range(groups)]
    pair = 2 * GMLP_GROUP_DIM
    lane = lax.broadcasted_iota(jnp.int32, (GMLP_CHUNK, 2 * pair), 1)
    first_group = (lane % pair) < GMLP_GROUP_DIM
    bs = bs_ref[...]
    for c in range(tm // (2 * GMLP_CHUNK)):
        r0 = 2 * c * GMLP_CHUNK
        r1 = r0 + GMLP_CHUNK
        pieces = ([], [])
        for p in range(groups // 2):
            vgp = jnp.concatenate([vg[r0:r1, p * pair:(p + 1) * pair],
                                   vg[r1:r1 + GMLP_CHUNK, p * pair:(p + 1) * pair]], axis=1)
            mix = jnp.where(first_group, _mm(w[2 * p], vgp), _mm(w[2 * p + 1], vgp))
            pieces[0].append(mix[:, :pair])
            pieces[1].append(mix[:, pair:])
        for e, lo in enumerate((r0, r1)):
            mixed = jnp.concatenate(pieces[e], axis=1) + bs
            a_ref[lo:lo + GMLP_CHUNK, :] = (u[lo:lo + GMLP_CHUNK] * mixed).astype(BF16)


def _even_in(x, g, wqt, wk, wvt, wf, fb, wz, lng, lnb, ws, bs):
    B, T, D = x.shape
    tm = TOKEN_TILE
    fw = wk.shape[1]
    gw = wz.shape[1] // 2
    tile = lambda w_: pl.BlockSpec((None, tm, w_), lambda b, t: (b, t, 0))
    tile_t = pl.BlockSpec((None, fw, tm), lambda b, t: (b, 0, t))
    shape_t = jax.ShapeDtypeStruct((B, fw, T), BF16)
    return pl.pallas_call(
        _even_in_kernel,
        grid=(B, T // tm),
        in_specs=[tile(D), _full(g), _full(wqt), _full(wk), _full(wvt), _full(wf), _full(fb),
                  _full(wz), _full(lng), _full(lnb), _full(ws), _full(bs)],
        out_specs=[tile_t, tile(fw), tile_t, tile(LANES_V7X), tile(gw)],
        out_shape=[shape_t, jax.ShapeDtypeStruct((B, T, fw), BF16), shape_t,
                   jax.ShapeDtypeStruct((B, T, LANES_V7X), F32),
                   jax.ShapeDtypeStruct((B, T, gw), BF16)],
        compiler_params=_params(2),
        name="even_in",
    )(x, g, wqt, wk, wvt, wf, fb, wz, lng, lnb, ws, bs)


def _fox_cum_kernel(lf_ref, ct_ref):
    x = lf_ref[...]
    T = x.shape[0]
    row = lax.broadcasted_iota(jnp.int32, x.shape, 0)
    shift = 1
    while shift < T:
        x = x + jnp.where(row >= shift, pltpu.roll(x, shift, axis=0), 0.0)
        shift *= 2
    ct_ref[...] = x * LOG2_E


def _fox_cum(lf):
    B, T, L = lf.shape
    spec = pl.BlockSpec((None, T, L), lambda b: (b, 0, 0))
    return pl.pallas_call(
        _fox_cum_kernel,
        grid=(B,),
        in_specs=[spec],
        out_specs=spec,
        out_shape=jax.ShapeDtypeStruct((B, T, L), F32),
        compiler_params=_params(1),
        name="fox_cum",
    )(lf)


def _fox_attn_kernel(qt_ref, k_ref, vt_ref, ct_ref, o_ref):
    hp = pl.program_id(1)
    T = k_ref.shape[0]
    tq = FOX_BLOCK
    hd = FOX_HEAD_DIM
    ct = ct_ref[...]
    lane = lax.broadcasted_iota(jnp.int32, ct.shape, 1)
    key = lax.broadcasted_iota(jnp.int32, (tq, tq), 0)
    qry = lax.broadcasted_iota(jnp.int32, (tq, tq), 1)
    causal = key <= qry
    zeros = jnp.zeros((hd, tq), BF16)
    ck = [jnp.sum(jnp.where(lane == 2 * hp + j, ct, 0.0), axis=1, keepdims=True) for j in range(2)]

    def scores(i, j):
        r0 = i * tq
        qt = qt_ref[j * hd:(j + 1) * hd, r0:r0 + tq]
        qt = jnp.concatenate([qt, zeros] if j == 0 else [zeros, qt], axis=0)
        s_d = _mm(k_ref[r0:r0 + tq, :], qt)
        s_o = _mm(k_ref[:r0, :], qt) if i > 0 else None
        return s_d, s_o

    def attend(i, j, s_d, s_o):
        r0 = i * tq
        s_d = jnp.where(causal, s_d - ck[j][r0:r0 + tq], -jnp.inf)
        m = jnp.max(s_d, axis=0, keepdims=True)
        if i > 0:
            s_o = s_o - ck[j][:r0]
            m = jnp.maximum(m, jnp.max(s_o, axis=0, keepdims=True))
        p_d = jnp.exp2(s_d - m)
        l = jnp.sum(p_d, axis=0, keepdims=True)
        acc = _mm(vt_ref[j * hd:(j + 1) * hd, r0:r0 + tq], p_d.astype(BF16))
        if i > 0:
            p_o = jnp.exp2(s_o - m)
            l = l + jnp.sum(p_o, axis=0, keepdims=True)
            acc = acc + _mm(vt_ref[j * hd:(j + 1) * hd, :r0], p_o.astype(BF16))
        return (acc * (1.0 / l)).T

    work = [(i, j) for i in range(T // tq) for j in range(2)]
    pending = scores(*work[0])
    outs = []
    for n, (i, j) in enumerate(work):
        current = pending
        if n + 1 < len(work):
            pending = scores(*work[n + 1])
        outs.append(attend(i, j, *current))
        if j == 1:
            o_ref[i * tq:(i + 1) * tq, :] = jnp.concatenate(outs, axis=1).astype(BF16)
            outs = []


def _fox_attn(qt, k, vt, ct):
    B, T, fw = k.shape
    pair = 2 * FOX_HEAD_DIM
    spec_t = pl.BlockSpec((None, pair, T), lambda b, h: (b, h, 0))
    spec = pl.BlockSpec((None, T, pair), lambda b, h: (b, 0, h))
    return pl.pallas_call(
        _fox_attn_kernel,
        grid=(B, fw // pair),
        in_specs=[spec_t, spec, spec_t,
                  pl.BlockSpec((None, T, ct.shape[2]), lambda b, h: (b, 0, 0))],
        out_specs=spec,
        out_shape=jax.ShapeDtypeStruct((B, T, fw), BF16),
        compiler_params=_params(2),
        name="fox_attn",
    )(qt, k, vt, ct)


def _mem_kv_kernel(m_ref, g_ref, wkv_ref, kt_ref, v_ref):
    m = _rmsnorm(m_ref[...], g_ref[...]).astype(BF16)
    kv = _mm(m, wkv_ref[...])
    d = kv.shape[1] // 2
    kt_ref[...] = kv[:, :d].T.astype(BF16)
    v_ref[...] = kv[:, d:].astype(BF16)


def _mem_kv(mem, g, wkv):
    B, M, D = mem.shape
    depth = wkv.shape[0]
    return pl.pallas_call(
        _mem_kv_kernel,
        grid=(depth, B),
        in_specs=[pl.BlockSpec((None, M, D), lambda l, b: (b, 0, 0)),
                  pl.BlockSpec((None, 1, D), lambda l, b: (l, 0, 0)),
                  pl.BlockSpec((None, D, 2 * D), lambda l, b: (l, 0, 0))],
        out_specs=[pl.BlockSpec((None, None, D, M), lambda l, b: (l, b, 0, 0)),
                   pl.BlockSpec((None, None, M, D), lambda l, b: (l, b, 0, 0))],
        out_shape=[jax.ShapeDtypeStruct((depth, B, D, M), BF16),
                   jax.ShapeDtypeStruct((depth, B, M, D), BF16)],
        compiler_params=_params(2),
        name="mem_kv",
    )(mem, g, wkv)


def _xattn_body(x, g_ref, wq_ref, kt_ref, v_ref, wo_ref):
    hd = wq_ref.shape[1] // XA_HEADS
    h = _rmsnorm(x, g_ref[...]).astype(BF16)
    q = (_mm(h, wq_ref[...]) * hd ** -0.5).astype(BF16)
    heads = [slice(hh * hd, (hh + 1) * hd) for hh in range(XA_HEADS)]
    scores = [_mm(q[:, sl], kt_ref[sl, :]) for sl in heads]
    outs = []
    for s, sl in zip(scores, heads):
        p = jnp.exp(s - jnp.max(s, axis=1, keepdims=True))
        p = (p / jnp.sum(p, axis=1, keepdims=True)).astype(BF16)
        outs.append(_mm(p, v_ref[:, sl]).astype(BF16))
    return x + _mm(jnp.concatenate(outs, axis=1), wo_ref[...])


def _xattn_kernel(x_ref, g_ref, wq_ref, kt_ref, v_ref, wo_ref, o_ref):
    o_ref[...] = _xattn_body(x_ref[...], g_ref, wq_ref, kt_ref, v_ref, wo_ref)


def _mix_xattn_kernel(x_ref, b_ref, a_ref, wout_ref, g_ref, wq_ref, kt_ref, v_ref, wo_ref, o_ref):
    fw = b_ref.shape[1]
    x = x_ref[...] + _mm(b_ref[...], wout_ref[:fw, :]) + _mm(a_ref[...], wout_ref[fw:, :])
    o_ref[...] = _xattn_body(x, g_ref, wq_ref, kt_ref, v_ref, wo_ref)


def _xattn(x, mix, g, wq, kt, v, wo):
    B, T, D = x.shape
    tm = TOKEN_TILE
    M = v.shape[1]
    tile = lambda w_: pl.BlockSpec((None, tm, w_), lambda b, t: (b, t, 0))
    kv_specs = [pl.BlockSpec((None, D, M), lambda b, t: (b, 0, 0)),
                pl.BlockSpec((None, M, D), lambda b, t: (b, 0, 0))]
    if mix is None:
        kern, pre_args, pre_specs = _xattn_kernel, (), []
    else:
        b_out, a_out, w_out = mix
        kern, pre_args = _mix_xattn_kernel, (b_out, a_out, w_out)
        pre_specs = [tile(b_out.shape[2]), tile(a_out.shape[2]), _full(w_out)]
    return pl.pallas_call(
        kern,
        grid=(B, T // tm),
        in_specs=[tile(D)] + pre_specs + [_full(g), _full(wq)] + kv_specs + [_full(wo)],
        out_specs=tile(D),
        out_shape=jax.ShapeDtypeStruct((B, T, D), F32),
        compiler_params=_params(2),
        name="xattn" if mix is None else "mix_xattn",
    )(x, *pre_args, g, wq, kt, v, wo)


def _ffn_kernel(x_ref, g_ref, wg_ref, wu_ref, wd_ref, fg_ref, o_ref, *, final):
    x = x_ref[...]
    h = _rmsnorm(x, g_ref[...]).astype(BF16)
    hidden = wg_ref.shape[1]
    hc = pl.cdiv(pl.cdiv(hidden, FFN_CHUNKS), MXU_TILE_V7X) * MXU_TILE_V7X
    y = x
    for lo in range(0, hidden, hc):
        hi = min(lo + hc, hidden)
        a = jax.nn.silu(_mm(h, wg_ref[:, lo:hi])) * _mm(h, wu_ref[:, lo:hi])
        y = y + _mm(a.astype(BF16), wd_ref[lo:hi, :])
    if final:
        y = _rmsnorm(y, fg_ref[...])
    o_ref[...] = y


def _ffn(x, g, wg, wu, wd, fg, *, final):
    B, T, D = x.shape
    tm = FFN_TOKEN_TILE
    tile = pl.BlockSpec((None, tm, D), lambda b, t: (b, t, 0))
    return pl.pallas_call(
        functools.partial(_ffn_kernel, final=final),
        grid=(B, T // tm),
        in_specs=[tile, _full(g), _full(wg), _full(wu), _full(wd), _full(fg)],
        out_specs=tile,
        out_shape=jax.ShapeDtypeStruct((B, T, D), F32),
        compiler_params=_params(2),
        name="ffn_final" if final else "ffn",
    )(x, g, wg, wu, wd, fg)


def _conv_kernel(x_ref, g_ref, win_ref, bin_ref, dww_ref, dwb_ref, lng_ref, lnb_ref, wout_ref,
                 bout_ref, o_ref, y_buf):
    x = x_ref[...]
    tm = x.shape[0]
    taps = dww_ref.shape[0]

    @pl.when(pl.program_id(1) == 0)
    def _():
        y_buf[:CONV_HALO, :] = jnp.zeros((CONV_HALO, y_buf.shape[1]), F32)

    h = _rmsnorm(x, g_ref[...]).astype(BF16)
    ag = _mm(h, win_ref[...]) + bin_ref[...]
    cw = ag.shape[1] // 2
    y_buf[CONV_HALO:, :] = ag[:, :cw] * jax.nn.sigmoid(ag[:, cw:])

    first = CONV_HALO - (taps - 1)
    n = CONV_HALO + tm
    yb = y_buf[...]
    acc = jnp.zeros((tm, cw), F32)
    for k in range(SUBLANES_V7X):
        shifted = yb if k == 0 else pltpu.roll(yb, n - k, axis=0)
        for a in range(CONV_HALO // SUBLANES_V7X + 1):
            j = SUBLANES_V7X * a + k - first
            if 0 <= j < taps:
                r0 = SUBLANES_V7X * a
                acc = acc + dww_ref[j:j + 1, :] * shifted[r0:r0 + tm, :]
    y_buf[:CONV_HALO, :] = y_buf[tm:tm + CONV_HALO, :]

    y = jax.nn.silu(_layernorm(acc + dwb_ref[...], lng_ref[...], lnb_ref[...]))
    o_ref[...] = x + _mm(y.astype(BF16), wout_ref[...]) + bout_ref[...]


def _conv(x, g, win, bin_, dww, dwb, lng, lnb, wout, bout):
    B, T, D = x.shape
    tm = TOKEN_TILE
    cw = wout.shape[0]
    tile = pl.BlockSpec((None, tm, D), lambda b, t: (b, t, 0))
    return pl.pallas_call(
        _conv_kernel,
        grid=(B, T // tm),
        in_specs=[tile, _full(g), _full(win), _full(bin_), _full(dww), _full(dwb), _full(lng),
                  _full(lnb), _full(wout), _full(bout)],
        out_specs=tile,
        out_shape=jax.ShapeDtypeStruct((B, T, D), F32),
        scratch_shapes=[pltpu.VMEM((CONV_HALO + tm, cw), F32)],
        compiler_params=_params(2),
        name="conv",
    )(x, g, win, bin_, dww, dwb, lng, lnb, wout, bout)


def kernel(x, mem, mix_norm_e, w_in_e, fox_f_bias, gmlp_ln_g, gmlp_ln_b, gmlp_w_s, gmlp_b_s, w_out_e, mix_norm_o, conv_w_in, conv_b_in, conv_dw_w, conv_dw_b, conv_ln_g, conv_ln_b, conv_w_out, conv_b_out, xa_norm, mem_norm, xa_wq, xa_wkv, xa_wo, ffn_norm, ffn_w_gu, ffn_w_down, final_norm):
    D = x.shape[2]
    depth = xa_wq.shape[0]
    heads = fox_f_bias.shape[1]
    fw = heads * FOX_HEAD_DIM
    groups = gmlp_w_s.shape[1]
    gw = groups * GMLP_GROUP_DIM
    hidden = ffn_w_down.shape[1]
    assert 2 * GMLP_GROUP_DIM == LANES_V7X and groups % 2 == 0 and heads % 2 == 0
    assert w_in_e.shape[2] == 3 * fw + heads + 2 * gw and conv_dw_w.shape[1] <= CONV_HALO + 1

    row = lambda p: p.reshape(1, -1)
    kt, vm = _mem_kv(mem, mem_norm.reshape(depth, 1, D), xa_wkv.astype(BF16))

    for layer in range(depth):
        li = layer // 2
        if layer % 2 == 0:
            w_in = w_in_e[li]
            wf = jnp.pad(w_in[:, 3 * fw:3 * fw + heads], ((0, 0), (0, LANES_V7X - heads)))
            fb = jnp.pad(fox_f_bias[li], (0, LANES_V7X - heads)).reshape(1, -1)
            bs = jnp.repeat(gmlp_b_s[li].T, GMLP_GROUP_DIM, axis=1)
            qt, k, vt, lf, a_out = _even_in(
                x, row(mix_norm_e[li]), w_in[:, :fw].T.astype(BF16),
                w_in[:, fw:2 * fw].astype(BF16), w_in[:, 2 * fw:3 * fw].T.astype(BF16),
                wf.astype(BF16), fb, w_in[:, 3 * fw + heads:].astype(BF16), row(gmlp_ln_g[li]),
                row(gmlp_ln_b[li]), gmlp_w_s[li], bs)
            b_out = _fox_attn(qt, k, vt, _fox_cum(lf))
            mix = (b_out, a_out, w_out_e[li].astype(BF16))
        else:
            x = _conv(x, row(mix_norm_o[li]), conv_w_in[li].astype(BF16), row(conv_b_in[li]),
                      conv_dw_w[li], row(conv_dw_b[li]), row(conv_ln_g[li]), row(conv_ln_b[li]),
                      conv_w_out[li].astype(BF16), row(conv_b_out[li]))
            mix = None
        x = _xattn(x, mix, row(xa_norm[layer]), xa_wq[layer].astype(BF16), kt[layer], vm[layer],
                   xa_wo[layer].astype(BF16))
        w_gu = ffn_w_gu[layer].astype(BF16)
        x = _ffn(x, row(ffn_norm[layer]), w_gu[:, :hidden], w_gu[:, hidden:],
                 ffn_w_down[layer].astype(BF16), row(final_norm), final=layer == depth - 1)
    return x
```

```python
import functools

import jax
import jax.numpy as jnp
from jax import lax
from jax.experimental import pallas as pl
from jax.experimental.pallas import tpu as pltpu

F32 = jnp.float32
BF16 = jnp.bfloat16
EPS = 1e-6
LOG2_E = 1.4426950408889634
NT_DIMS = (((1,), (1,)), ((), ()))

LANES_V7X = 128
SUBLANES_V7X = 8
MXU_TILE_V7X = 256
VMEM_LIMIT_V7X = 56 * 1024 * 1024

FOX_HEAD_DIM = 64
GMLP_GROUP_DIM = 64
GMLP_CHUNK = 128
XA_HEADS = 4
XA_ROW_CHUNKS = 2
CONV_HALO = 32

TOKEN_TILE = 512
FFN_TOKEN_TILE = 1024
FOX_BLOCK = 256
FFN_CHUNKS = 2


def _rmsnorm(x, g):
    return x * lax.rsqrt(jnp.mean(x * x, axis=-1, keepdims=True) + EPS) * g


def _layernorm(x, g, b):
    mu = jnp.mean(x, axis=-1, keepdims=True)
    xc = x - mu
    return xc * lax.rsqrt(jnp.mean(xc * xc, axis=-1, keepdims=True) + EPS) * g + b


def _mm(a, b):
    return jnp.dot(a, b, preferred_element_type=F32)


def _full(a):
    nd = a.ndim
    return pl.BlockSpec(a.shape, lambda *_: (0,) * nd, pipeline_mode=pl.Buffered(1))


def _params(n_grid):
    return pltpu.CompilerParams(
        dimension_semantics=("arbitrary",) * n_grid, vmem_limit_bytes=VMEM_LIMIT_V7X)


def _even_in_kernel(x_ref, g_ref, wqt_ref, wk_ref, wvt_ref, wf_ref, fb_ref, wz_ref, lng_ref, lnb_ref,
                    ws_ref, bs_ref, qt_ref, k_ref, vt_ref, lf_ref, a_ref):
    x = x_ref[...]
    tm = x.shape[0]
    h = _rmsnorm(x, g_ref[...]).astype(BF16)

    z = jax.nn.gelu(_mm(h, wz_ref[...]))
    gw = z.shape[1] // 2
    u = z[:, :gw]
    vg = _layernorm(z[:, gw:], lng_ref[...], lnb_ref[...]).astype(BF16)

    scale = FOX_HEAD_DIM ** -0.5 * LOG2_E
    qt_ref[...] = (lax.dot_general(wqt_ref[...], h, NT_DIMS, preferred_element_type=F32)
                   * scale).astype(BF16)
    k_ref[...] = _mm(h, wk_ref[...]).astype(BF16)
    vt_ref[...] = lax.dot_general(wvt_ref[...], h, NT_DIMS,
                                  preferred_element_type=F32).astype(BF16)

    lf_ref[...] = jax.nn.log_sigmoid(_mm(h, wf_ref[...]) + fb_ref[...])

    groups = gw // GMLP_GROUP_DIM
    row = lax.broadcasted_iota(jnp.int32, (GMLP_CHUNK, GMLP_CHUNK), 0)
    col = lax.broadcasted_iota(jnp.int32, (GMLP_CHUNK, GMLP_CHUNK), 1)
    causal = row >= col
    w = [jnp.where(causal, ws_ref[g], 0.0).astype(BF16) for g in range(groups)]
    pair = 2 * GMLP_GROUP_DIM
    lane = lax.broadcasted_iota(jnp.int32, (GMLP_CHUNK, 2 * pair), 1)
    first_group = (lane % pair) < GMLP_GROUP_DIM
    bs = bs_ref[...]
    for c in range(tm // (2 * GMLP_CHUNK)):
        r0 = 2 * c * GMLP_CHUNK
        r1 = r0 + GMLP_CHUNK
        pieces = ([], [])
        for p in range(groups // 2):
            vgp = jnp.concatenate([vg[r0:r1, p * pair:(p + 1) * pair],
                                   vg[r1:r1 + GMLP_CHUNK, p * pair:(p + 1) * pair]], axis=1)
            mix = jnp.where(first_group, _mm(w[2 * p], vgp), _mm(w[2 * p + 1], vgp))
            pieces[0].append(mix[:, :pair])
            pieces[1].append(mix[:, pair:])
        for e, lo in enumerate((r0, r1)):
            mixed = jnp.concatenate(pieces[e], axis=1) + bs
            a_ref[lo:lo + GMLP_CHUNK, :] = (u[lo:lo + GMLP_CHUNK] * mixed).astype(BF16)


def _even_in(x, g, wqt, wk, wvt, wf, fb, wz, lng, lnb, ws, bs):
    B, T, D = x.shape
    tm = FFN_TOKEN_TILE
    fw = wk.shape[1]
    gw = wz.shape[1] // 2
    tile = lambda w_: pl.BlockSpec((None, tm, w_), lambda b, t: (b, t, 0))
    tile_t = pl.BlockSpec((None, fw, tm), lambda b, t: (b, 0, t))
    shape_t = jax.ShapeDtypeStruct((B, fw, T), BF16)
    return pl.pallas_call(
        _even_in_kernel,
        grid=(B, T // tm),
        in_specs=[tile(D), _full(g), _full(wqt), _full(wk), _full(wvt), _full(wf), _full(fb),
                  _full(wz), _full(lng), _full(lnb), _full(ws), _full(bs)],
        out_specs=[tile_t, tile(fw), tile_t, tile(LANES_V7X), tile(gw)],
        out_shape=[shape_t, jax.ShapeDtypeStruct((B, T, fw), BF16), shape_t,
                   jax.ShapeDtypeStruct((B, T, LANES_V7X), F32),
                   jax.ShapeDtypeStruct((B, T, gw), BF16)],
        compiler_params=_params(2),
        name="even_in",
    )(x, g, wqt, wk, wvt, wf, fb, wz, lng, lnb, ws, bs)


def _fox_cum_kernel(lf_ref, ct_ref):
    x = lf_ref[...]
    T = x.shape[0]
    row = lax.broadcasted_iota(jnp.int32, x.shape, 0)
    shift = 1
    while shift < T:
        x = x + jnp.where(row >= shift, pltpu.roll(x, shift, axis=0), 0.0)
        shift *= 2
    ct_ref[...] = x * LOG2_E


def _fox_cum(lf):
    B, T, L = lf.shape
    spec = pl.BlockSpec((None, T, L), lambda b: (b, 0, 0))
    return pl.pallas_call(
        _fox_cum_kernel,
        grid=(B,),
        in_specs=[spec],
        out_specs=spec,
        out_shape=jax.ShapeDtypeStruct((B, T, L), F32),
        compiler_params=_params(1),
        name="fox_cum",
    )(lf)


def _fox_attn_kernel(qt_ref, k_ref, vt_ref, ct_ref, o_ref):
    hp = pl.program_id(1)
    T = k_ref.shape[0]
    tq = FOX_BLOCK
    hd = FOX_HEAD_DIM
    ct = ct_ref[...]
    lane = lax.broadcasted_iota(jnp.int32, ct.shape, 1)
    key = lax.broadcasted_iota(jnp.int32, (tq, tq), 0)
    qry = lax.broadcasted_iota(jnp.int32, (tq, tq), 1)
    causal = key <= qry
    zeros = jnp.zeros((hd, tq), BF16)
    ck = [jnp.sum(jnp.where(lane == 2 * hp + j, ct, 0.0), axis=1, keepdims=True) for j in range(2)]

    def scores(i, j):
        r0 = i * tq
        qt = qt_ref[j * hd:(j + 1) * hd, r0:r0 + tq]
        qt = jnp.concatenate([qt, zeros] if j == 0 else [zeros, qt], axis=0)
        s_d = _mm(k_ref[r0:r0 + tq, :], qt)
        s_o = _mm(k_ref[:r0, :], qt) if i > 0 else None
        return s_d, s_o

    def attend(i, j, s_d, s_o):
        r0 = i * tq
        s_d = jnp.where(causal, s_d - ck[j][r0:r0 + tq], -jnp.inf)
        m = jnp.max(s_d, axis=0, keepdims=True)
        if i > 0:
            s_o = s_o - ck[j][:r0]
            m = jnp.maximum(m, jnp.max(s_o, axis=0, keepdims=True))
        p_d = jnp.exp2(s_d - m)
        l = jnp.sum(p_d, axis=0, keepdims=True)
        acc = _mm(vt_ref[j * hd:(j + 1) * hd, r0:r0 + tq], p_d.astype(BF16))
        if i > 0:
            p_o = jnp.exp2(s_o - m)
            l = l + jnp.sum(p_o, axis=0, keepdims=True)
            acc = acc + _mm(vt_ref[j * hd:(j + 1) * hd, :r0], p_o.astype(BF16))
        return (acc * (1.0 / l)).T

    work = [(i, j) for i in range(T // tq) for j in range(2)]
    pending = scores(*work[0])
    outs = []
    for n, (i, j) in enumerate(work):
        current = pending
        if n + 1 < len(work):
            pending = scores(*work[n + 1])
        outs.append(attend(i, j, *current))
        if j == 1:
            o_ref[i * tq:(i + 1) * tq, :] = jnp.concatenate(outs, axis=1).astype(BF16)
            outs = []


def _fox_attn(qt, k, vt, ct):
    B, T, fw = k.shape
    pair = 2 * FOX_HEAD_DIM
    spec_t = pl.BlockSpec((None, pair, T), lambda b, h: (b, h, 0))
    spec = pl.BlockSpec((None, T, pair), lambda b, h: (b, 0, h))
    return pl.pallas_call(
        _fox_attn_kernel,
        grid=(B, fw // pair),
        in_specs=[spec_t, spec, spec_t,
                  pl.BlockSpec((None, T, ct.shape[2]), lambda b, h: (b, 0, 0))],
        out_specs=spec,
        out_shape=jax.ShapeDtypeStruct((B, T, fw), BF16),
        compiler_params=_params(2),
        name="fox_attn",
    )(qt, k, vt, ct)


def _mem_kv_kernel(m_ref, g_ref, wkv_ref, kt_ref, v_ref):
    m = _rmsnorm(m_ref[...], g_ref[...]).astype(BF16)
    kv = _mm(m, wkv_ref[...])
    d = kv.shape[1] // 2
    kt_ref[...] = kv[:, :d].T.astype(BF16)
    v_ref[...] = kv[:, d:].astype(BF16)


def _mem_kv(mem, g, wkv):
    B, M, D = mem.shape
    depth = wkv.shape[0]
    return pl.pallas_call(
        _mem_kv_kernel,
        grid=(depth, B),
        in_specs=[pl.BlockSpec((None, M, D), lambda l, b: (b, 0, 0)),
                  pl.BlockSpec((None, 1, D), lambda l, b: (l, 0, 0)),
                  pl.BlockSpec((None, D, 2 * D), lambda l, b: (l, 0, 0))],
        out_specs=[pl.BlockSpec((None, None, D, M), lambda l, b: (l, b, 0, 0)),
                   pl.BlockSpec((None, None, M, D), lambda l, b: (l, b, 0, 0))],
        out_shape=[jax.ShapeDtypeStruct((depth, B, D, M), BF16),
                   jax.ShapeDtypeStruct((depth, B, M, D), BF16)],
        compiler_params=_params(2),
        name="mem_kv",
    )(mem, g, wkv)


def _xattn_body(xs, g_ref, wq_ref, kt_ref, v_ref, wo_ref):
    hd = wq_ref.shape[1] // XA_HEADS
    heads = [slice(hh * hd, (hh + 1) * hd) for hh in range(XA_HEADS)]
    hs = [_rmsnorm(x, g_ref[...]).astype(BF16) for x in xs]
    qs = [(_mm(h, wq_ref[...]) * hd ** -0.5).astype(BF16) for h in hs]
    scores = [[_mm(q[:, sl], kt_ref[sl, :]) for sl in heads] for q in qs]
    ys = []
    for x, chunk_scores in zip(xs, scores):
        outs = []
        for s, sl in zip(chunk_scores, heads):
            p = jnp.exp(s - jnp.max(s, axis=1, keepdims=True))
            p = (p / jnp.sum(p, axis=1, keepdims=True)).astype(BF16)
            outs.append(_mm(p, v_ref[:, sl]).astype(BF16))
        ys.append(x + _mm(jnp.concatenate(outs, axis=1), wo_ref[...]))
    return ys


def _row_chunks(n):
    rows = n // XA_ROW_CHUNKS
    return [slice(c * rows, (c + 1) * rows) for c in range(XA_ROW_CHUNKS)]


def _xattn_kernel(x_ref, g_ref, wq_ref, kt_ref, v_ref, wo_ref, o_ref):
    chunks = _row_chunks(x_ref.shape[0])
    ys = _xattn_body([x_ref[c, :] for c in chunks], g_ref, wq_ref, kt_ref, v_ref, wo_ref)
    for c, y in zip(chunks, ys):
        o_ref[c, :] = y


def _mix_xattn_kernel(x_ref, b_ref, a_ref, wout_ref, g_ref, wq_ref, kt_ref, v_ref, wo_ref, o_ref):
    fw = b_ref.shape[1]
    chunks = _row_chunks(x_ref.shape[0])
    xs = [x_ref[c, :] + _mm(b_ref[c, :], wout_ref[:fw, :]) + _mm(a_ref[c, :], wout_ref[fw:, :])
          for c in chunks]
    ys = _xattn_body(xs, g_ref, wq_ref, kt_ref, v_ref, wo_ref)
    for c, y in zip(chunks, ys):
        o_ref[c, :] = y


def _xattn(x, mix, g, wq, kt, v, wo):
    B, T, D = x.shape
    tm = FFN_TOKEN_TILE
    M = v.shape[1]
    tile = lambda w_: pl.BlockSpec((None, tm, w_), lambda b, t: (b, t, 0))
    kv_specs = [pl.BlockSpec((None, D, M), lambda b, t: (b, 0, 0)),
                pl.BlockSpec((None, M, D), lambda b, t: (b, 0, 0))]
    if mix is None:
        kern, pre_args, pre_specs = _xattn_kernel, (), []
    else:
        b_out, a_out, w_out = mix
        kern, pre_args = _mix_xattn_kernel, (b_out, a_out, w_out)
        pre_specs = [tile(b_out.shape[2]), tile(a_out.shape[2]), _full(w_out)]
    return pl.pallas_call(
        kern,
        grid=(B, T // tm),
        in_specs=[tile(D)] + pre_specs + [_full(g), _full(wq)] + kv_specs + [_full(wo)],
        out_specs=tile(D),
        out_shape=jax.ShapeDtypeStruct((B, T, D), F32),
        compiler_params=_params(2),
        name="xattn" if mix is None else "mix_xattn",
    )(x, *pre_args, g, wq, kt, v, wo)


def _ffn_kernel(x_ref, g_ref, wgu_ref, wd_ref, fg_ref, o_ref, *, final):
    x = x_ref[...]
    h = _rmsnorm(x, g_ref[...]).astype(BF16)
    hidden = wd_ref.shape[0]
    hc = pl.cdiv(pl.cdiv(hidden, FFN_CHUNKS), MXU_TILE_V7X) * MXU_TILE_V7X
    y = x
    for lo in range(0, hidden, hc):
        hi = min(lo + hc, hidden)
        a = jax.nn.silu(_mm(h, wgu_ref[:, lo:hi])) * _mm(h, wgu_ref[:, hidden + lo:hidden + hi])
        y = y + _mm(a.astype(BF16), wd_ref[lo:hi, :])
    if final:
        y = _rmsnorm(y, fg_ref[...])
    o_ref[...] = y


def _ffn(x, g, wgu, wd, fg, *, final):
    B, T, D = x.shape
    tm = FFN_TOKEN_TILE
    tile = pl.BlockSpec((None, tm, D), lambda b, t: (b, t, 0))
    return pl.pallas_call(
        functools.partial(_ffn_kernel, final=final),
        grid=(B, T // tm),
        in_specs=[tile, _full(g), _full(wgu), _full(wd), _full(fg)],
        out_specs=tile,
        out_shape=jax.ShapeDtypeStruct((B, T, D), F32),
        compiler_params=_params(2),
        name="ffn_final" if final else "ffn",
    )(x, g, wgu, wd, fg)


def _conv_kernel(x_ref, g_ref, win_ref, bin_ref, dww_ref, dwb_ref, lng_ref, lnb_ref, wout_ref,
                 bout_ref, o_ref, y_buf):
    x = x_ref[...]
    tm = x.shape[0]
    taps = dww_ref.shape[0]

    @pl.when(pl.program_id(1) == 0)
    def _():
        y_buf[:CONV_HALO, :] = jnp.zeros((CONV_HALO, y_buf.shape[1]), F32)

    h = _rmsnorm(x, g_ref[...]).astype(BF16)
    ag = _mm(h, win_ref[...]) + bin_ref[...]
    cw = ag.shape[1] // 2
    y_buf[CONV_HALO:, :] = ag[:, :cw] * jax.nn.sigmoid(ag[:, cw:])

    first = CONV_HALO - (taps - 1)
    n = CONV_HALO + tm
    yb = y_buf[...]
    acc = jnp.zeros((tm, cw), F32)
    for k in range(SUBLANES_V7X):
        shifted = yb if k == 0 else pltpu.roll(yb, n - k, axis=0)
        for a in range(CONV_HALO // SUBLANES_V7X + 1):
            j = SUBLANES_V7X * a + k - first
            if 0 <= j < taps:
                r0 = SUBLANES_V7X * a
                acc = acc + dww_ref[j:j + 1, :] * shifted[r0:r0 + tm, :]
    y_buf[:CONV_HALO, :] = y_buf[tm:tm + CONV_HALO, :]

    y = jax.nn.silu(_layernorm(acc + dwb_ref[...], lng_ref[...], lnb_ref[...]))
    o_ref[...] = x + _mm(y.astype(BF16), wout_ref[...]) + bout_ref[...]


def _conv(x, g, win, bin_, dww, dwb, lng, lnb, wout, bout):
    B, T, D = x.shape
    tm = TOKEN_TILE
    cw = wout.shape[0]
    tile = pl.BlockSpec((None, tm, D), lambda b, t: (b, t, 0))
    return pl.pallas_call(
        _conv_kernel,
        grid=(B, T // tm),
        in_specs=[tile, _full(g), _full(win), _full(bin_), _full(dww), _full(dwb), _full(lng),
                  _full(lnb), _full(wout), _full(bout)],
        out_specs=tile,
        out_shape=jax.ShapeDtypeStruct((B, T, D), F32),
        scratch_shapes=[pltpu.VMEM((CONV_HALO + tm, cw), F32)],
        compiler_params=_params(2),
        name="conv",
    )(x, g, win, bin_, dww, dwb, lng, lnb, wout, bout)


def kernel(x, mem, mix_norm_e, w_in_e, fox_f_bias, gmlp_ln_g, gmlp_ln_b, gmlp_w_s, gmlp_b_s, w_out_e, mix_norm_o, conv_w_in, conv_b_in, conv_dw_w, conv_dw_b, conv_ln_g, conv_ln_b, conv_w_out, conv_b_out, xa_norm, mem_norm, xa_wq, xa_wkv, xa_wo, ffn_norm, ffn_w_gu, ffn_w_down, final_norm):
    D = x.shape[2]
    depth = xa_wq.shape[0]
    heads = fox_f_bias.shape[1]
    fw = heads * FOX_HEAD_DIM
    groups = gmlp_w_s.shape[1]
    gw = groups * GMLP_GROUP_DIM
    assert 2 * GMLP_GROUP_DIM == LANES_V7X and groups % 2 == 0 and heads % 2 == 0
    assert w_in_e.shape[2] == 3 * fw + heads + 2 * gw and conv_dw_w.shape[1] <= CONV_HALO + 1

    row = lambda p: p.reshape(1, -1)
    kt, vm = _mem_kv(mem, mem_norm.reshape(depth, 1, D), xa_wkv.astype(BF16))

    for layer in range(depth):
        li = layer // 2
        if layer % 2 == 0:
            w_in = w_in_e[li]
            wf = jnp.pad(w_in[:, 3 * fw:3 * fw + heads], ((0, 0), (0, LANES_V7X - heads)))
            fb = jnp.pad(fox_f_bias[li], (0, LANES_V7X - heads)).reshape(1, -1)
            bs = jnp.repeat(gmlp_b_s[li].T, GMLP_GROUP_DIM, axis=1)
            qt, k, vt, lf, a_out = _even_in(
                x, row(mix_norm_e[li]), w_in[:, :fw].T.astype(BF16),
                w_in[:, fw:2 * fw].astype(BF16), w_in[:, 2 * fw:3 * fw].T.astype(BF16),
                wf.astype(BF16), fb, w_in[:, 3 * fw + heads:].astype(BF16), row(gmlp_ln_g[li]),
                row(gmlp_ln_b[li]), gmlp_w_s[li], bs)
            b_out = _fox_attn(qt, k, vt, _fox_cum(lf))
            mix = (b_out, a_out, w_out_e[li].astype(BF16))
        else:
            x = _conv(x, row(mix_norm_o[li]), conv_w_in[li].astype(BF16), row(conv_b_in[li]),
                      conv_dw_w[li], row(conv_dw_b[li]), row(conv_ln_g[li]), row(conv_ln_b[li]),
                      conv_w_out[li].astype(BF16), row(conv_b_out[li]))
            mix = None
        x = _xattn(x, mix, row(xa_norm[layer]), xa_wq[layer].astype(BF16), kt[layer], vm[layer],
                   xa_wo[layer].astype(BF16))
        x = _ffn(x, row(ffn_norm[layer]), ffn_w_gu[layer].astype(BF16),
                 ffn_w_down[layer].astype(BF16), row(final_norm), final=layer == depth - 1)
    return x
```

```python
import functools

import jax
import jax.numpy as jnp
from jax import lax
from jax.experimental import pallas as pl
from jax.experimental.pallas import tpu as pltpu

F32 = jnp.float32
BF16 = jnp.bfloat16
EPS = 1e-6
LOG2_E = 1.4426950408889634
NT_DIMS = (((1,), (1,)), ((), ()))

LANES_V7X = 128
SUBLANES_V7X = 8
MXU_TILE_V7X = 256
VMEM_LIMIT_V7X = 56 * 1024 * 1024

FOX_HEAD_DIM = 64
GMLP_GROUP_DIM = 64
GMLP_CHUNK = 128
XA_HEADS = 4
XA_ROW_CHUNKS = 2
CONV_HALO = 32

TOKEN_TILE = 512
FFN_TOKEN_TILE = 1024
FOX_BLOCK = 256
FFN_CHUNKS = 2


def _rmsnorm(x, g):
    return x * lax.rsqrt(jnp.mean(x * x, axis=-1, keepdims=True) + EPS) * g


def _layernorm(x, g, b):
    mu = jnp.mean(x, axis=-1, keepdims=True)
    xc = x - mu
    return xc * lax.rsqrt(jnp.mean(xc * xc, axis=-1, keepdims=True) + EPS) * g + b


def _mm(a, b):
    return jnp.dot(a, b, preferred_element_type=F32)


def _full(a):
    nd = a.ndim
    return pl.BlockSpec(a.shape, lambda *_: (0,) * nd, pipeline_mode=pl.Buffered(1))


def _params(n_grid):
    return pltpu.CompilerParams(
        dimension_semantics=("arbitrary",) * n_grid, vmem_limit_bytes=VMEM_LIMIT_V7X)


def _even_in_kernel(x_ref, g_ref, wqt_ref, wk_ref, wvt_ref, wf_ref, fb_ref, wz_ref, lng_ref, lnb_ref,
                    ws_ref, bs_ref, qt_ref, k_ref, vt_ref, ct_ref, a_ref, carry_ref):
    @pl.when(pl.program_id(1) == 0)
    def _():
        carry_ref[...] = jnp.zeros(carry_ref.shape, F32)

    x = x_ref[...]
    tm = x.shape[0]
    h = _rmsnorm(x, g_ref[...]).astype(BF16)

    cum = jax.nn.log_sigmoid(_mm(h, wf_ref[...]) + fb_ref[...])
    pos = lax.broadcasted_iota(jnp.int32, cum.shape, 0)
    shift = 1
    while shift < tm:
        cum = cum + jnp.where(pos >= shift, pltpu.roll(cum, shift, axis=0), 0.0)
        shift *= 2
    cum = cum + carry_ref[0:1, :]
    carry_ref[...] = jnp.broadcast_to(cum[tm - 1:tm, :], carry_ref.shape)
    ct_ref[...] = cum * LOG2_E

    z = jax.nn.gelu(_mm(h, wz_ref[...]))
    gw = z.shape[1] // 2
    u = z[:, :gw]
    vg = _layernorm(z[:, gw:], lng_ref[...], lnb_ref[...]).astype(BF16)

    scale = FOX_HEAD_DIM ** -0.5 * LOG2_E
    qt_ref[...] = (lax.dot_general(wqt_ref[...], h, NT_DIMS, preferred_element_type=F32)
                   * scale).astype(BF16)
    k_ref[...] = _mm(h, wk_ref[...]).astype(BF16)
    vt_ref[...] = lax.dot_general(wvt_ref[...], h, NT_DIMS,
                                  preferred_element_type=F32).astype(BF16)


    groups = gw // GMLP_GROUP_DIM
    row = lax.broadcasted_iota(jnp.int32, (GMLP_CHUNK, GMLP_CHUNK), 0)
    col = lax.broadcasted_iota(jnp.int32, (GMLP_CHUNK, GMLP_CHUNK), 1)
    causal = row >= col
    w = [jnp.where(causal, ws_ref[g], 0.0).astype(BF16) for g in range(groups)]
    pair = 2 * GMLP_GROUP_DIM
    lane = lax.broadcasted_iota(jnp.int32, (GMLP_CHUNK, 2 * pair), 1)
    first_group = (lane % pair) < GMLP_GROUP_DIM
    bs = bs_ref[...]
    for c in range(tm // (2 * GMLP_CHUNK)):
        r0 = 2 * c * GMLP_CHUNK
        r1 = r0 + GMLP_CHUNK
        pieces = ([], [])
        for p in range(groups // 2):
            vgp = jnp.concatenate([vg[r0:r1, p * pair:(p + 1) * pair],
                                   vg[r1:r1 + GMLP_CHUNK, p * pair:(p + 1) * pair]], axis=1)
            mix = jnp.where(first_group, _mm(w[2 * p], vgp), _mm(w[2 * p + 1], vgp))
            pieces[0].append(mix[:, :pair])
            pieces[1].append(mix[:, pair:])
        for e, lo in enumerate((r0, r1)):
            mixed = jnp.concatenate(pieces[e], axis=1) + bs
            a_ref[lo:lo + GMLP_CHUNK, :] = (u[lo:lo + GMLP_CHUNK] * mixed).astype(BF16)


def _even_in(x, g, wqt, wk, wvt, wf, fb, wz, lng, lnb, ws, bs):
    B, T, D = x.shape
    tm = FFN_TOKEN_TILE
    fw = wk.shape[1]
    gw = wz.shape[1] // 2
    tile = lambda w_: pl.BlockSpec((None, tm, w_), lambda b, t: (b, t, 0))
    tile_t = pl.BlockSpec((None, fw, tm), lambda b, t: (b, 0, t))
    shape_t = jax.ShapeDtypeStruct((B, fw, T), BF16)
    return pl.pallas_call(
        _even_in_kernel,
        grid=(B, T // tm),
        in_specs=[tile(D), _full(g), _full(wqt), _full(wk), _full(wvt), _full(wf), _full(fb),
                  _full(wz), _full(lng), _full(lnb), _full(ws), _full(bs)],
        out_specs=[tile_t, tile(fw), tile_t, tile(LANES_V7X), tile(gw)],
        out_shape=[shape_t, jax.ShapeDtypeStruct((B, T, fw), BF16), shape_t,
                   jax.ShapeDtypeStruct((B, T, LANES_V7X), F32),
                   jax.ShapeDtypeStruct((B, T, gw), BF16)],
        scratch_shapes=[pltpu.VMEM((SUBLANES_V7X, LANES_V7X), F32)],
        compiler_params=_params(2),
        name="even_in",
    )(x, g, wqt, wk, wvt, wf, fb, wz, lng, lnb, ws, bs)


def _fox_attn_kernel(qt_ref, k_ref, vt_ref, ct_ref, o_ref):
    hp = pl.program_id(1)
    T = k_ref.shape[0]
    tq = FOX_BLOCK
    hd = FOX_HEAD_DIM
    ct = ct_ref[...]
    lane = lax.broadcasted_iota(jnp.int32, ct.shape, 1)
    key = lax.broadcasted_iota(jnp.int32, (tq, tq), 0)
    qry = lax.broadcasted_iota(jnp.int32, (tq, tq), 1)
    causal = key <= qry
    zeros = jnp.zeros((hd, tq), BF16)
    ck = [jnp.sum(jnp.where(lane == 2 * hp + j, ct, 0.0), axis=1, keepdims=True) for j in range(2)]

    def scores(i, j):
        r0 = i * tq
        qt = qt_ref[j * hd:(j + 1) * hd, r0:r0 + tq]
        qt = jnp.concatenate([qt, zeros] if j == 0 else [zeros, qt], axis=0)
        s_d = _mm(k_ref[r0:r0 + tq, :], qt)
        s_o = _mm(k_ref[:r0, :], qt) if i > 0 else None
        return s_d, s_o

    def attend(i, j, s_d, s_o):
        r0 = i * tq
        s_d = jnp.where(causal, s_d - ck[j][r0:r0 + tq], -jnp.inf)
        m = jnp.max(s_d, axis=0, keepdims=True)
        if i > 0:
            s_o = s_o - ck[j][:r0]
            m = jnp.maximum(m, jnp.max(s_o, axis=0, keepdims=True))
        p_d = jnp.exp2(s_d - m)
        l = jnp.sum(p_d, axis=0, keepdims=True)
        acc = _mm(vt_ref[j * hd:(j + 1) * hd, r0:r0 + tq], p_d.astype(BF16))
        if i > 0:
            p_o = jnp.exp2(s_o - m)
            l = l + jnp.sum(p_o, axis=0, keepdims=True)
            acc = acc + _mm(vt_ref[j * hd:(j + 1) * hd, :r0], p_o.astype(BF16))
        return (acc * (1.0 / l)).T

    work = [(i, j) for i in reversed(range(T // tq)) for j in range(2)]
    pending = scores(*work[0])
    outs = []
    for n, (i, j) in enumerate(work):
        current = pending
        if n + 1 < len(work):
            pending = scores(*work[n + 1])
        outs.append(attend(i, j, *current))
        if j == 1:
            o_ref[i * tq:(i + 1) * tq, :] = jnp.concatenate(outs, axis=1).astype(BF16)
            outs = []


def _fox_attn(qt, k, vt, ct):
    B, T, fw = k.shape
    pair = 2 * FOX_HEAD_DIM
    spec_t = pl.BlockSpec((None, pair, T), lambda b, h: (b, h, 0))
    spec = pl.BlockSpec((None, T, pair), lambda b, h: (b, 0, h))
    return pl.pallas_call(
        _fox_attn_kernel,
        grid=(B, fw // pair),
        in_specs=[spec_t, spec, spec_t,
                  pl.BlockSpec((None, T, ct.shape[2]), lambda b, h: (b, 0, 0))],
        out_specs=spec,
        out_shape=jax.ShapeDtypeStruct((B, T, fw), BF16),
        compiler_params=_params(2),
        name="fox_attn",
    )(qt, k, vt, ct)


def _mem_kv_kernel(m_ref, g_ref, wkv_ref, kt_ref, v_ref):
    m = _rmsnorm(m_ref[...], g_ref[...]).astype(BF16)
    kv = _mm(m, wkv_ref[...])
    d = kv.shape[1] // 2
    kt_ref[...] = kv[:, :d].T.astype(BF16)
    v_ref[...] = kv[:, d:].astype(BF16)


def _mem_kv(mem, g, wkv):
    B, M, D = mem.shape
    depth = wkv.shape[0]
    return pl.pallas_call(
        _mem_kv_kernel,
        grid=(depth, B),
        in_specs=[pl.BlockSpec((None, M, D), lambda l, b: (b, 0, 0)),
                  pl.BlockSpec((None, 1, D), lambda l, b: (l, 0, 0)),
                  pl.BlockSpec((None, D, 2 * D), lambda l, b: (l, 0, 0))],
        out_specs=[pl.BlockSpec((None, None, D, M), lambda l, b: (l, b, 0, 0)),
                   pl.BlockSpec((None, None, M, D), lambda l, b: (l, b, 0, 0))],
        out_shape=[jax.ShapeDtypeStruct((depth, B, D, M), BF16),
                   jax.ShapeDtypeStruct((depth, B, M, D), BF16)],
        compiler_params=_params(2),
        name="mem_kv",
    )(mem, g, wkv)


def _xattn_body(xs, g_ref, wq_ref, kt_ref, v_ref, wo_ref):
    hd = wq_ref.shape[1] // XA_HEADS
    heads = [slice(hh * hd, (hh + 1) * hd) for hh in range(XA_HEADS)]
    hs = [_rmsnorm(x, g_ref[...]).astype(BF16) for x in xs]
    qs = [(_mm(h, wq_ref[...]) * hd ** -0.5).astype(BF16) for h in hs]
    scores = [[_mm(q[:, sl], kt_ref[sl, :]) for sl in heads] for q in qs]
    ys = []
    for x, chunk_scores in zip(xs, scores):
        outs = []
        for s, sl in zip(chunk_scores, heads):
            p = jnp.exp(s - jnp.max(s, axis=1, keepdims=True))
            p = (p / jnp.sum(p, axis=1, keepdims=True)).astype(BF16)
            outs.append(_mm(p, v_ref[:, sl]).astype(BF16))
        ys.append(x + _mm(jnp.concatenate(outs, axis=1), wo_ref[...]))
    return ys


def _row_chunks(n):
    rows = n // XA_ROW_CHUNKS
    return [slice(c * rows, (c + 1) * rows) for c in range(XA_ROW_CHUNKS)]


def _xattn_kernel(x_ref, g_ref, wq_ref, kt_ref, v_ref, wo_ref, o_ref):
    chunks = _row_chunks(x_ref.shape[0])
    ys = _xattn_body([x_ref[c, :] for c in chunks], g_ref, wq_ref, kt_ref, v_ref, wo_ref)
    for c, y in zip(chunks, ys):
        o_ref[c, :] = y


def _mix_xattn_kernel(x_ref, b_ref, a_ref, wout_ref, g_ref, wq_ref, kt_ref, v_ref, wo_ref, o_ref):
    fw = b_ref.shape[1]
    chunks = _row_chunks(x_ref.shape[0])
    xs = [x_ref[c, :] + _mm(b_ref[c, :], wout_ref[:fw, :]) + _mm(a_ref[c, :], wout_ref[fw:, :])
          for c in chunks]
    ys = _xattn_body(xs, g_ref, wq_ref, kt_ref, v_ref, wo_ref)
    for c, y in zip(chunks, ys):
        o_ref[c, :] = y


def _xattn(x, mix, g, wq, kt, v, wo):
    B, T, D = x.shape
    tm = FFN_TOKEN_TILE
    M = v.shape[1]
    tile = lambda w_: pl.BlockSpec((None, tm, w_), lambda b, t: (b, t, 0))
    kv_specs = [pl.BlockSpec((None, D, M), lambda b, t: (b, 0, 0)),
                pl.BlockSpec((None, M, D), lambda b, t: (b, 0, 0))]
    if mix is None:
        kern, pre_args, pre_specs = _xattn_kernel, (), []
    else:
        b_out, a_out, w_out = mix
        kern, pre_args = _mix_xattn_kernel, (b_out, a_out, w_out)
        pre_specs = [tile(b_out.shape[2]), tile(a_out.shape[2]), _full(w_out)]
    return pl.pallas_call(
        kern,
        grid=(B, T // tm),
        in_specs=[tile(D)] + pre_specs + [_full(g), _full(wq)] + kv_specs + [_full(wo)],
        out_specs=tile(D),
        out_shape=jax.ShapeDtypeStruct((B, T, D), F32),
        compiler_params=_params(2),
        name="xattn" if mix is None else "mix_xattn",
    )(x, *pre_args, g, wq, kt, v, wo)


def _ffn_kernel(x_ref, g_ref, wgu_ref, wd_ref, fg_ref, o_ref, *, final):
    x = x_ref[...]
    h = _rmsnorm(x, g_ref[...]).astype(BF16)
    hidden = wd_ref.shape[0]
    hc = pl.cdiv(pl.cdiv(hidden, FFN_CHUNKS), MXU_TILE_V7X) * MXU_TILE_V7X
    y = x
    for lo in range(0, hidden, hc):
        hi = min(lo + hc, hidden)
        a = jax.nn.silu(_mm(h, wgu_ref[:, lo:hi])) * _mm(h, wgu_ref[:, hidden + lo:hidden + hi])
        y = y + _mm(a.astype(BF16), wd_ref[lo:hi, :])
    if final:
        y = _rmsnorm(y, fg_ref[...])
    o_ref[...] = y


def _ffn(x, g, wgu, wd, fg, *, final):
    B, T, D = x.shape
    tm = FFN_TOKEN_TILE
    tile = pl.BlockSpec((None, tm, D), lambda b, t: (b, t, 0))
    return pl.pallas_call(
        functools.partial(_ffn_kernel, final=final),
        grid=(B, T // tm),
        in_specs=[tile, _full(g), _full(wgu), _full(wd), _full(fg)],
        out_specs=tile,
        out_shape=jax.ShapeDtypeStruct((B, T, D), F32),
        compiler_params=_params(2),
        name="ffn_final" if final else "ffn",
    )(x, g, wgu, wd, fg)


def _conv_kernel(x_ref, g_ref, win_ref, bin_ref, dww_ref, dwb_ref, lng_ref, lnb_ref, wout_ref,
                 bout_ref, o_ref, y_buf):
    x = x_ref[...]
    tm = x.shape[0]
    taps = dww_ref.shape[0]

    @pl.when(pl.program_id(1) == 0)
    def _():
        y_buf[:CONV_HALO, :] = jnp.zeros((CONV_HALO, y_buf.shape[1]), F32)

    h = _rmsnorm(x, g_ref[...]).astype(BF16)
    ag = _mm(h, win_ref[...]) + bin_ref[...]
    cw = ag.shape[1] // 2
    y_buf[CONV_HALO:, :] = ag[:, :cw] * jax.nn.sigmoid(ag[:, cw:])

    first = CONV_HALO - (taps - 1)
    n = CONV_HALO + tm
    yb = y_buf[...]
    acc = None
    for k in range(SUBLANES_V7X):
        shifted = yb if k == 0 else pltpu.roll(yb, n - k, axis=0)
        for a in range(CONV_HALO // SUBLANES_V7X + 1):
            j = SUBLANES_V7X * a + k - first
            if 0 <= j < taps:
                r0 = SUBLANES_V7X * a
                term = dww_ref[j:j + 1, :] * shifted[r0:r0 + tm, :]
                acc = term if acc is None else acc + term
    y_buf[:CONV_HALO, :] = y_buf[tm:tm + CONV_HALO, :]

    y = jax.nn.silu(_layernorm(acc + dwb_ref[...], lng_ref[...], lnb_ref[...]))
    o_ref[...] = x + _mm(y.astype(BF16), wout_ref[...]) + bout_ref[...]


def _conv(x, g, win, bin_, dww, dwb, lng, lnb, wout, bout):
    B, T, D = x.shape
    tm = TOKEN_TILE
    cw = wout.shape[0]
    tile = pl.BlockSpec((None, tm, D), lambda b, t: (b, t, 0))
    return pl.pallas_call(
        _conv_kernel,
        grid=(B, T // tm),
        in_specs=[tile, _full(g), _full(win), _full(bin_), _full(dww), _full(dwb), _full(lng),
                  _full(lnb), _full(wout), _full(bout)],
        out_specs=tile,
        out_shape=jax.ShapeDtypeStruct((B, T, D), F32),
        scratch_shapes=[pltpu.VMEM((CONV_HALO + tm, cw), F32)],
        compiler_params=_params(2),
        name="conv",
    )(x, g, win, bin_, dww, dwb, lng, lnb, wout, bout)


def kernel(x, mem, mix_norm_e, w_in_e, fox_f_bias, gmlp_ln_g, gmlp_ln_b, gmlp_w_s, gmlp_b_s, w_out_e, mix_norm_o, conv_w_in, conv_b_in, conv_dw_w, conv_dw_b, conv_ln_g, conv_ln_b, conv_w_out, conv_b_out, xa_norm, mem_norm, xa_wq, xa_wkv, xa_wo, ffn_norm, ffn_w_gu, ffn_w_down, final_norm):
    D = x.shape[2]
    depth = xa_wq.shape[0]
    heads = fox_f_bias.shape[1]
    fw = heads * FOX_HEAD_DIM
    groups = gmlp_w_s.shape[1]
    gw = groups * GMLP_GROUP_DIM
    assert 2 * GMLP_GROUP_DIM == LANES_V7X and groups % 2 == 0 and heads % 2 == 0
    assert w_in_e.shape[2] == 3 * fw + heads + 2 * gw and conv_dw_w.shape[1] <= CONV_HALO + 1

    row = lambda p: p.reshape(1, -1)
    kt, vm = _mem_kv(mem, mem_norm.reshape(depth, 1, D), xa_wkv.astype(BF16))

    for layer in range(depth):
        li = layer // 2
        if layer % 2 == 0:
            w_in = w_in_e[li]
            wf = jnp.pad(w_in[:, 3 * fw:3 * fw + heads], ((0, 0), (0, LANES_V7X - heads)))
            fb = jnp.pad(fox_f_bias[li], (0, LANES_V7X - heads)).reshape(1, -1)
            bs = jnp.repeat(gmlp_b_s[li].T, GMLP_GROUP_DIM, axis=1)
            qt, k, vt, ct, a_out = _even_in(
                x, row(mix_norm_e[li]), w_in[:, :fw].T.astype(BF16),
                w_in[:, fw:2 * fw].astype(BF16), w_in[:, 2 * fw:3 * fw].T.astype(BF16),
                wf.astype(BF16), fb, w_in[:, 3 * fw + heads:].astype(BF16), row(gmlp_ln_g[li]),
                row(gmlp_ln_b[li]), gmlp_w_s[li], bs)
            b_out = _fox_attn(qt, k, vt, ct)
            mix = (b_out, a_out, w_out_e[li].astype(BF16))
        else:
            x = _conv(x, row(mix_norm_o[li]), conv_w_in[li].astype(BF16), row(conv_b_in[li]),
                      conv_dw_w[li], row(conv_dw_b[li]), row(conv_ln_g[li]), row(conv_ln_b[li]),
                      conv_w_out[li].astype(BF16), row(conv_b_out[li]))
            mix = None
        x = _xattn(x, mix, row(xa_norm[layer]), xa_wq[layer].astype(BF16), kt[layer], vm[layer],
                   xa_wo[layer].astype(BF16))
        x = _ffn(x, row(ffn_norm[layer]), ffn_w_gu[layer].astype(BF16),
                 ffn_w_down[layer].astype(BF16), row(final_norm), final=layer == depth - 1)
    return x
```

```python
import functools

import jax
import jax.numpy as jnp
from jax import lax
from jax.experimental import pallas as pl
from jax.experimental.pallas import tpu as pltpu

F32 = jnp.float32
BF16 = jnp.bfloat16
EPS = 1e-6
LOG2_E = 1.4426950408889634
NT_DIMS = (((1,), (1,)), ((), ()))

LANES_V7X = 128
SUBLANES_V7X = 8
MXU_TILE_V7X = 256
VMEM_LIMIT_V7X = 56 * 1024 * 1024

FOX_HEAD_DIM = 64
GMLP_GROUP_DIM = 64
GMLP_CHUNK = 128
XA_HEADS = 4
XA_ROW_CHUNKS = 2
CONV_HALO = 32

TOKEN_TILE = 512
FFN_TOKEN_TILE = 1024
FOX_BLOCK = 256
FFN_CHUNKS = 2


def _rmsnorm(x, g):
    return x * lax.rsqrt(jnp.mean(x * x, axis=-1, keepdims=True) + EPS) * g


def _layernorm(x, g, b):
    mu = jnp.mean(x, axis=-1, keepdims=True)
    xc = x - mu
    return xc * lax.rsqrt(jnp.mean(xc * xc, axis=-1, keepdims=True) + EPS) * g + b


def _mm(a, b):
    return jnp.dot(a, b, preferred_element_type=F32)


def _full(a):
    nd = a.ndim
    return pl.BlockSpec(a.shape, lambda *_: (0,) * nd, pipeline_mode=pl.Buffered(1))


def _params(n_grid):
    return pltpu.CompilerParams(
        dimension_semantics=("arbitrary",) * n_grid, vmem_limit_bytes=VMEM_LIMIT_V7X)


def _even_in_kernel(x_ref, g_ref, wqt_ref, wk_ref, wvt_ref, wf_ref, fb_ref, wz_ref, lng_ref, lnb_ref,
                    ws_ref, bs_ref, qt_ref, k_ref, vt_ref, ct_ref, a_ref, carry_ref):
    @pl.when(pl.program_id(1) == 0)
    def _():
        carry_ref[...] = jnp.zeros(carry_ref.shape, F32)

    x = x_ref[...]
    tm = x.shape[0]
    h = _rmsnorm(x, g_ref[...]).astype(BF16)

    cum = jax.nn.log_sigmoid(_mm(h, wf_ref[...]) + fb_ref[...])
    pos = lax.broadcasted_iota(jnp.int32, cum.shape, 0)
    shift = 1
    while shift < tm:
        cum = cum + jnp.where(pos >= shift, pltpu.roll(cum, shift, axis=0), 0.0)
        shift *= 2
    cum = cum + carry_ref[0:1, :]
    carry_ref[...] = jnp.broadcast_to(cum[tm - 1:tm, :], carry_ref.shape)
    ct_ref[...] = cum * LOG2_E

    z = jax.nn.gelu(_mm(h, wz_ref[...]))
    gw = z.shape[1] // 2
    u = z[:, :gw]
    vg = _layernorm(z[:, gw:], lng_ref[...], lnb_ref[...]).astype(BF16)

    scale = FOX_HEAD_DIM ** -0.5 * LOG2_E
    qt_ref[...] = (lax.dot_general(wqt_ref[...], h, NT_DIMS, preferred_element_type=F32)
                   * scale).astype(BF16)
    k_ref[...] = _mm(h, wk_ref[...]).astype(BF16)
    vt_ref[...] = lax.dot_general(wvt_ref[...], h, NT_DIMS,
                                  preferred_element_type=F32).astype(BF16)


    groups = gw // GMLP_GROUP_DIM
    row = lax.broadcasted_iota(jnp.int32, (GMLP_CHUNK, GMLP_CHUNK), 0)
    col = lax.broadcasted_iota(jnp.int32, (GMLP_CHUNK, GMLP_CHUNK), 1)
    causal = row >= col
    w = [jnp.where(causal, ws_ref[g], 0.0).astype(BF16) for g in range(groups)]
    pair = 2 * GMLP_GROUP_DIM
    lane = lax.broadcasted_iota(jnp.int32, (GMLP_CHUNK, 2 * pair), 1)
    first_group = (lane % pair) < GMLP_GROUP_DIM
    bs = bs_ref[...]
    for c in range(tm // (2 * GMLP_CHUNK)):
        r0 = 2 * c * GMLP_CHUNK
        r1 = r0 + GMLP_CHUNK
        pieces = ([], [])
        for p in range(groups // 2):
            vgp = jnp.concatenate([vg[r0:r1, p * pair:(p + 1) * pair],
                                   vg[r1:r1 + GMLP_CHUNK, p * pair:(p + 1) * pair]], axis=1)
            mix = jnp.where(first_group, _mm(w[2 * p], vgp), _mm(w[2 * p + 1], vgp))
            pieces[0].append(mix[:, :pair])
            pieces[1].append(mix[:, pair:])
        for e, lo in enumerate((r0, r1)):
            mixed = jnp.concatenate(pieces[e], axis=1) + bs
            a_ref[lo:lo + GMLP_CHUNK, :] = (u[lo:lo + GMLP_CHUNK] * mixed).astype(BF16)


def _even_in(x, g, wqt, wk, wvt, wf, fb, wz, lng, lnb, ws, bs):
    B, T, D = x.shape
    tm = FFN_TOKEN_TILE
    fw = wk.shape[1]
    gw = wz.shape[1] // 2
    tile = lambda w_: pl.BlockSpec((None, tm, w_), lambda b, t: (b, t, 0))
    tile_t = pl.BlockSpec((None, fw, tm), lambda b, t: (b, 0, t))
    shape_t = jax.ShapeDtypeStruct((B, fw, T), BF16)
    return pl.pallas_call(
        _even_in_kernel,
        grid=(B, T // tm),
        in_specs=[tile(D), _full(g), _full(wqt), _full(wk), _full(wvt), _full(wf), _full(fb),
                  _full(wz), _full(lng), _full(lnb), _full(ws), _full(bs)],
        out_specs=[tile_t, tile(fw), tile_t, tile(LANES_V7X), tile(gw)],
        out_shape=[shape_t, jax.ShapeDtypeStruct((B, T, fw), BF16), shape_t,
                   jax.ShapeDtypeStruct((B, T, LANES_V7X), F32),
                   jax.ShapeDtypeStruct((B, T, gw), BF16)],
        scratch_shapes=[pltpu.VMEM((SUBLANES_V7X, LANES_V7X), F32)],
        compiler_params=_params(2),
        name="even_in",
    )(x, g, wqt, wk, wvt, wf, fb, wz, lng, lnb, ws, bs)


def _fox_attn_kernel(qt_ref, k_ref, vt_ref, ct_ref, o_ref):
    hp = pl.program_id(1)
    T = k_ref.shape[0]
    tq = FOX_BLOCK
    hd = FOX_HEAD_DIM
    ct = ct_ref[...]
    lane = lax.broadcasted_iota(jnp.int32, ct.shape, 1)
    key = lax.broadcasted_iota(jnp.int32, (tq, tq), 0)
    qry = lax.broadcasted_iota(jnp.int32, (tq, tq), 1)
    causal = key <= qry
    zeros = jnp.zeros((hd, tq), BF16)
    ck = [jnp.sum(jnp.where(lane == 2 * hp + j, ct, 0.0), axis=1, keepdims=True) for j in range(2)]

    def scores(i, j):
        r0 = i * tq
        qt = qt_ref[j * hd:(j + 1) * hd, r0:r0 + tq]
        qt = jnp.concatenate([qt, zeros] if j == 0 else [zeros, qt], axis=0)
        s_d = _mm(k_ref[r0:r0 + tq, :], qt)
        s_o = _mm(k_ref[:r0, :], qt) if i > 0 else None
        return s_d, s_o

    def attend(i, j, s_d, s_o):
        r0 = i * tq
        s_d = jnp.where(causal, s_d - ck[j][r0:r0 + tq], -jnp.inf)
        m = jnp.max(s_d, axis=0, keepdims=True)
        if i > 0:
            s_o = s_o - ck[j][:r0]
            m = jnp.maximum(m, jnp.max(s_o, axis=0, keepdims=True))
        p_d = jnp.exp2(s_d - m)
        l = jnp.sum(p_d, axis=0, keepdims=True)
        acc = _mm(vt_ref[j * hd:(j + 1) * hd, r0:r0 + tq], p_d.astype(BF16))
        if i > 0:
            p_o = jnp.exp2(s_o - m)
            l = l + jnp.sum(p_o, axis=0, keepdims=True)
            acc = acc + _mm(vt_ref[j * hd:(j + 1) * hd, :r0], p_o.astype(BF16))
        return (acc * (1.0 / l)).T

    work = [(i, j) for i in reversed(range(T // tq)) for j in range(2)]
    pending = scores(*work[0])
    outs = []
    for n, (i, j) in enumerate(work):
        current = pending
        if n + 1 < len(work):
            pending = scores(*work[n + 1])
        outs.append(attend(i, j, *current))
        if j == 1:
            o_ref[i * tq:(i + 1) * tq, :] = jnp.concatenate(outs, axis=1).astype(BF16)
            outs = []


def _fox_attn(qt, k, vt, ct):
    B, T, fw = k.shape
    pair = 2 * FOX_HEAD_DIM
    spec_t = pl.BlockSpec((None, pair, T), lambda b, h: (b, h, 0))
    spec = pl.BlockSpec((None, T, pair), lambda b, h: (b, 0, h))
    return pl.pallas_call(
        _fox_attn_kernel,
        grid=(B, fw // pair),
        in_specs=[spec_t, spec, spec_t,
                  pl.BlockSpec((None, T, ct.shape[2]), lambda b, h: (b, 0, 0))],
        out_specs=spec,
        out_shape=jax.ShapeDtypeStruct((B, T, fw), BF16),
        compiler_params=_params(2),
        name="fox_attn",
    )(qt, k, vt, ct)


def _mem_kv_kernel(m_ref, g_ref, wkv_ref, kt_ref, v_ref):
    m = _rmsnorm(m_ref[...], g_ref[...]).astype(BF16)
    kv = _mm(m, wkv_ref[...])
    d = kv.shape[1] // 2
    kt_ref[...] = kv[:, :d].T.astype(BF16)
    v_ref[...] = kv[:, d:].astype(BF16)


def _mem_kv(mem, g, wkv):
    B, M, D = mem.shape
    depth = wkv.shape[0]
    return pl.pallas_call(
        _mem_kv_kernel,
        grid=(depth, B),
        in_specs=[pl.BlockSpec((None, M, D), lambda l, b: (b, 0, 0)),
                  pl.BlockSpec((None, 1, D), lambda l, b: (l, 0, 0)),
                  pl.BlockSpec((None, D, 2 * D), lambda l, b: (l, 0, 0))],
        out_specs=[pl.BlockSpec((None, None, D, M), lambda l, b: (l, b, 0, 0)),
                   pl.BlockSpec((None, None, M, D), lambda l, b: (l, b, 0, 0))],
        out_shape=[jax.ShapeDtypeStruct((depth, B, D, M), BF16),
                   jax.ShapeDtypeStruct((depth, B, M, D), BF16)],
        compiler_params=_params(2),
        name="mem_kv",
    )(mem, g, wkv)


def _xattn_body(xs, g_ref, wq_ref, kt_ref, v_ref, wo_ref):
    hd = wq_ref.shape[1] // XA_HEADS
    heads = [slice(hh * hd, (hh + 1) * hd) for hh in range(XA_HEADS)]
    hs = [_rmsnorm(x, g_ref[...]).astype(BF16) for x in xs]
    qs = [(_mm(h, wq_ref[...]) * hd ** -0.5).astype(BF16) for h in hs]
    scores = [[_mm(q[:, sl], kt_ref[sl, :]) for sl in heads] for q in qs]
    ys = []
    for x, chunk_scores in zip(xs, scores):
        outs = []
        for s, sl in zip(chunk_scores, heads):
            p = jnp.exp(s - jnp.max(s, axis=1, keepdims=True))
            p = (p / jnp.sum(p, axis=1, keepdims=True)).astype(BF16)
            outs.append(_mm(p, v_ref[:, sl]).astype(BF16))
        ys.append(x + _mm(jnp.concatenate(outs, axis=1), wo_ref[...]))
    return ys


def _row_chunks(n):
    rows = n // XA_ROW_CHUNKS
    return [slice(c * rows, (c + 1) * rows) for c in range(XA_ROW_CHUNKS)]


def _xattn_kernel(x_ref, g_ref, wq_ref, kt_ref, v_ref, wo_ref, o_ref):
    chunks = _row_chunks(x_ref.shape[0])
    ys = _xattn_body([x_ref[c, :] for c in chunks], g_ref, wq_ref, kt_ref, v_ref, wo_ref)
    for c, y in zip(chunks, ys):
        o_ref[c, :] = y


def _mix_xattn_kernel(x_ref, b_ref, a_ref, wout_ref, g_ref, wq_ref, kt_ref, v_ref, wo_ref, o_ref):
    fw = b_ref.shape[1]
    chunks = _row_chunks(x_ref.shape[0])
    xs = [x_ref[c, :] + _mm(b_ref[c, :], wout_ref[:fw, :]) + _mm(a_ref[c, :], wout_ref[fw:, :])
          for c in chunks]
    ys = _xattn_body(xs, g_ref, wq_ref, kt_ref, v_ref, wo_ref)
    for c, y in zip(chunks, ys):
        o_ref[c, :] = y


def _xattn(x, mix, g, wq, kt, v, wo):
    B, T, D = x.shape
    tm = FFN_TOKEN_TILE
    M = v.shape[1]
    tile = lambda w_: pl.BlockSpec((None, tm, w_), lambda b, t: (b, t, 0))
    kv_specs = [pl.BlockSpec((None, D, M), lambda b, t: (b, 0, 0)),
                pl.BlockSpec((None, M, D), lambda b, t: (b, 0, 0))]
    if mix is None:
        kern, pre_args, pre_specs = _xattn_kernel, (), []
    else:
        b_out, a_out, w_out = mix
        kern, pre_args = _mix_xattn_kernel, (b_out, a_out, w_out)
        pre_specs = [tile(b_out.shape[2]), tile(a_out.shape[2]), _full(w_out)]
    return pl.pallas_call(
        kern,
        grid=(B, T // tm),
        in_specs=[tile(D)] + pre_specs + [_full(g), _full(wq)] + kv_specs + [_full(wo)],
        out_specs=tile(D),
        out_shape=jax.ShapeDtypeStruct((B, T, D), F32),
        compiler_params=_params(2),
        name="xattn" if mix is None else "mix_xattn",
    )(x, *pre_args, g, wq, kt, v, wo)


def _ffn_kernel(x_ref, g_ref, wgu_ref, wd_ref, fg_ref, o_ref, *, final):
    x = x_ref[...]
    h = _rmsnorm(x, g_ref[...]).astype(BF16)
    hidden = wd_ref.shape[0]
    hc = pl.cdiv(pl.cdiv(hidden, FFN_CHUNKS), MXU_TILE_V7X) * MXU_TILE_V7X
    y = x
    for lo in range(0, hidden, hc):
        hi = min(lo + hc, hidden)
        a = jax.nn.silu(_mm(h, wgu_ref[:, lo:hi])) * _mm(h, wgu_ref[:, hidden + lo:hidden + hi])
        y = y + _mm(a.astype(BF16), wd_ref[lo:hi, :])
    if final:
        y = _rmsnorm(y, fg_ref[...])
    o_ref[...] = y


def _ffn(x, g, wgu, wd, fg, *, final):
    B, T, D = x.shape
    tm = FFN_TOKEN_TILE
    tile = pl.BlockSpec((None, tm, D), lambda b, t: (b, t, 0))
    return pl.pallas_call(
        functools.partial(_ffn_kernel, final=final),
        grid=(B, T // tm),
        in_specs=[tile, _full(g), _full(wgu), _full(wd), _full(fg)],
        out_specs=tile,
        out_shape=jax.ShapeDtypeStruct((B, T, D), F32),
        compiler_params=_params(2),
        name="ffn_final" if final else "ffn",
    )(x, g, wgu, wd, fg)


def _conv_kernel(x_ref, g_ref, win_ref, bin_ref, dww_ref, dwb_ref, lng_ref, lnb_ref, wout_ref,
                 bout_ref, o_ref, y_buf):
    x = x_ref[...]
    tm = x.shape[0]
    taps = dww_ref.shape[0]
    tiles = y_buf.shape[1]

    @pl.when(pl.program_id(1) == 0)
    def _():
        y_buf[:CONV_HALO] = jnp.zeros((CONV_HALO,) + y_buf.shape[1:], F32)

    h = _rmsnorm(x, g_ref[...]).astype(BF16)
    ag = _mm(h, win_ref[...]) + bin_ref[...]
    cw = ag.shape[1] // 2
    glu = ag[:, :cw] * jax.nn.sigmoid(ag[:, cw:])
    y_buf[CONV_HALO:] = pltpu.einshape("m(cl)->mcl", glu, c=tiles)

    first = CONV_HALO - (taps - 1)
    acc = None
    for j in range(taps):
        term = dww_ref[j] * y_buf[first + j:first + j + tm]
        acc = term if acc is None else acc + term
    y_buf[:CONV_HALO] = y_buf[tm:tm + CONV_HALO]

    acc = pltpu.einshape("mcl->m(cl)", acc)
    y = jax.nn.silu(_layernorm(acc + dwb_ref[...], lng_ref[...], lnb_ref[...]))
    o_ref[...] = x + _mm(y.astype(BF16), wout_ref[...]) + bout_ref[...]


def _conv(x, g, win, bin_, dww, dwb, lng, lnb, wout, bout):
    B, T, D = x.shape
    tm = TOKEN_TILE
    cw = wout.shape[0]
    tiles = cw // LANES_V7X
    tiled = lambda p: p.reshape(p.shape[0], tiles, LANES_V7X)
    dww = tiled(dww)
    tile = pl.BlockSpec((None, tm, D), lambda b, t: (b, t, 0))
    return pl.pallas_call(
        _conv_kernel,
        grid=(B, T // tm),
        in_specs=[tile, _full(g), _full(win), _full(bin_), _full(dww), _full(dwb), _full(lng),
                  _full(lnb), _full(wout), _full(bout)],
        out_specs=tile,
        out_shape=jax.ShapeDtypeStruct((B, T, D), F32),
        scratch_shapes=[pltpu.VMEM((CONV_HALO + tm, tiles, LANES_V7X), F32)],
        compiler_params=_params(2),
        name="conv",
    )(x, g, win, bin_, dww, dwb, lng, lnb, wout, bout)


def kernel(x, mem, mix_norm_e, w_in_e, fox_f_bias, gmlp_ln_g, gmlp_ln_b, gmlp_w_s, gmlp_b_s, w_out_e, mix_norm_o, conv_w_in, conv_b_in, conv_dw_w, conv_dw_b, conv_ln_g, conv_ln_b, conv_w_out, conv_b_out, xa_norm, mem_norm, xa_wq, xa_wkv, xa_wo, ffn_norm, ffn_w_gu, ffn_w_down, final_norm):
    D = x.shape[2]
    depth = xa_wq.shape[0]
    heads = fox_f_bias.shape[1]
    fw = heads * FOX_HEAD_DIM
    groups = gmlp_w_s.shape[1]
    gw = groups * GMLP_GROUP_DIM
    assert 2 * GMLP_GROUP_DIM == LANES_V7X and groups % 2 == 0 and heads % 2 == 0
    assert w_in_e.shape[2] == 3 * fw + heads + 2 * gw and conv_dw_w.shape[1] <= CONV_HALO + 1

    row = lambda p: p.reshape(1, -1)
    kt, vm = _mem_kv(mem, mem_norm.reshape(depth, 1, D), xa_wkv.astype(BF16))

    for layer in range(depth):
        li = layer // 2
        if layer % 2 == 0:
            w_in = w_in_e[li]
            wf = jnp.pad(w_in[:, 3 * fw:3 * fw + heads], ((0, 0), (0, LANES_V7X - heads)))
            fb = jnp.pad(fox_f_bias[li], (0, LANES_V7X - heads)).reshape(1, -1)
            bs = jnp.repeat(gmlp_b_s[li].T, GMLP_GROUP_DIM, axis=1)
            qt, k, vt, ct, a_out = _even_in(
                x, row(mix_norm_e[li]), w_in[:, :fw].T.astype(BF16),
                w_in[:, fw:2 * fw].astype(BF16), w_in[:, 2 * fw:3 * fw].T.astype(BF16),
                wf.astype(BF16), fb, w_in[:, 3 * fw + heads:].astype(BF16), row(gmlp_ln_g[li]),
                row(gmlp_ln_b[li]), gmlp_w_s[li], bs)
            b_out = _fox_attn(qt, k, vt, ct)
            mix = (b_out, a_out, w_out_e[li].astype(BF16))
        else:
            x = _conv(x, row(mix_norm_o[li]), conv_w_in[li].astype(BF16), row(conv_b_in[li]),
                      conv_dw_w[li], row(conv_dw_b[li]), row(conv_ln_g[li]), row(conv_ln_b[li]),
                      conv_w_out[li].astype(BF16), row(conv_b_out[li]))
            mix = None
        x = _xattn(x, mix, row(xa_norm[layer]), xa_wq[layer].astype(BF16), kt[layer], vm[layer],
                   xa_wo[layer].astype(BF16))
        x = _ffn(x, row(ffn_norm[layer]), ffn_w_gu[layer].astype(BF16),
                 ffn_w_down[layer].astype(BF16), row(final_norm), final=layer == depth - 1)
    return x
```

```python
import functools

import jax
import jax.numpy as jnp
from jax import lax
from jax.experimental import pallas as pl
from jax.experimental.pallas import tpu as pltpu

F32 = jnp.float32
BF16 = jnp.bfloat16
EPS = 1e-6
LOG2_E = 1.4426950408889634
NT_DIMS = (((1,), (1,)), ((), ()))

LANES_V7X = 128
SUBLANES_V7X = 8
MXU_TILE_V7X = 256
VMEM_LIMIT_V7X = 56 * 1024 * 1024

FOX_HEAD_DIM = 64
GMLP_GROUP_DIM = 64
GMLP_CHUNK = 128
XA_HEADS = 4
XA_ROW_CHUNKS = 2
CONV_HALO = 32
CONV_ROW_CHUNKS = 2

TOKEN_TILE = 512
FFN_TOKEN_TILE = 1024
FOX_BLOCK = 256
FFN_CHUNKS = 2


def _rmsnorm(x, g):
    return x * lax.rsqrt(jnp.mean(x * x, axis=-1, keepdims=True) + EPS) * g


def _layernorm(x, g, b):
    mu = jnp.mean(x, axis=-1, keepdims=True)
    xc = x - mu
    return xc * lax.rsqrt(jnp.mean(xc * xc, axis=-1, keepdims=True) + EPS) * g + b


def _mm(a, b):
    return jnp.dot(a, b, preferred_element_type=F32)


def _full(a):
    nd = a.ndim
    return pl.BlockSpec(a.shape, lambda *_: (0,) * nd, pipeline_mode=pl.Buffered(1))


def _params(n_grid):
    return pltpu.CompilerParams(
        dimension_semantics=("arbitrary",) * n_grid, vmem_limit_bytes=VMEM_LIMIT_V7X)


def _even_in_kernel(x_ref, g_ref, wqt_ref, wk_ref, wvt_ref, wf_ref, fb_ref, wz_ref, lng_ref, lnb_ref,
                    ws_ref, bs_ref, qt_ref, k_ref, vt_ref, ct_ref, a_ref, carry_ref):
    @pl.when(pl.program_id(1) == 0)
    def _():
        carry_ref[...] = jnp.zeros(carry_ref.shape, F32)

    x = x_ref[...]
    tm = x.shape[0]
    h = _rmsnorm(x, g_ref[...]).astype(BF16)

    cum = jax.nn.log_sigmoid(_mm(h, wf_ref[...]) + fb_ref[...])
    pos = lax.broadcasted_iota(jnp.int32, cum.shape, 0)
    shift = 1
    while shift < tm:
        cum = cum + jnp.where(pos >= shift, pltpu.roll(cum, shift, axis=0), 0.0)
        shift *= 2
    cum = cum + carry_ref[0:1, :]
    carry_ref[...] = jnp.broadcast_to(cum[tm - 1:tm, :], carry_ref.shape)
    ct_ref[...] = cum * LOG2_E

    z = jax.nn.gelu(_mm(h, wz_ref[...]))
    gw = z.shape[1] // 2
    u = z[:, :gw]
    vg = _layernorm(z[:, gw:], lng_ref[...], lnb_ref[...]).astype(BF16)

    scale = FOX_HEAD_DIM ** -0.5 * LOG2_E
    qt_ref[...] = (lax.dot_general(wqt_ref[...], h, NT_DIMS, preferred_element_type=F32)
                   * scale).astype(BF16)
    k_ref[...] = _mm(h, wk_ref[...]).astype(BF16)
    vt_ref[...] = lax.dot_general(wvt_ref[...], h, NT_DIMS,
                                  preferred_element_type=F32).astype(BF16)


    groups = gw // GMLP_GROUP_DIM
    row = lax.broadcasted_iota(jnp.int32, (GMLP_CHUNK, GMLP_CHUNK), 0)
    col = lax.broadcasted_iota(jnp.int32, (GMLP_CHUNK, GMLP_CHUNK), 1)
    causal = row >= col
    w = [jnp.where(causal, ws_ref[g], 0.0).astype(BF16) for g in range(groups)]
    pair = 2 * GMLP_GROUP_DIM
    lane = lax.broadcasted_iota(jnp.int32, (GMLP_CHUNK, 2 * pair), 1)
    first_group = (lane % pair) < GMLP_GROUP_DIM
    bs = bs_ref[...]
    for c in range(tm // (2 * GMLP_CHUNK)):
        r0 = 2 * c * GMLP_CHUNK
        r1 = r0 + GMLP_CHUNK
        pieces = ([], [])
        for p in range(groups // 2):
            vgp = jnp.concatenate([vg[r0:r1, p * pair:(p + 1) * pair],
                                   vg[r1:r1 + GMLP_CHUNK, p * pair:(p + 1) * pair]], axis=1)
            mix = jnp.where(first_group, _mm(w[2 * p], vgp), _mm(w[2 * p + 1], vgp))
            pieces[0].append(mix[:, :pair])
            pieces[1].append(mix[:, pair:])
        for e, lo in enumerate((r0, r1)):
            mixed = jnp.concatenate(pieces[e], axis=1) + bs
            a_ref[lo:lo + GMLP_CHUNK, :] = (u[lo:lo + GMLP_CHUNK] * mixed).astype(BF16)


def _even_in(x, g, wqt, wk, wvt, wf, fb, wz, lng, lnb, ws, bs):
    B, T, D = x.shape
    tm = FFN_TOKEN_TILE
    fw = wk.shape[1]
    gw = wz.shape[1] // 2
    tile = lambda w_: pl.BlockSpec((None, tm, w_), lambda b, t: (b, t, 0))
    tile_t = pl.BlockSpec((None, fw, tm), lambda b, t: (b, 0, t))
    shape_t = jax.ShapeDtypeStruct((B, fw, T), BF16)
    return pl.pallas_call(
        _even_in_kernel,
        grid=(B, T // tm),
        in_specs=[tile(D), _full(g), _full(wqt), _full(wk), _full(wvt), _full(wf), _full(fb),
                  _full(wz), _full(lng), _full(lnb), _full(ws), _full(bs)],
        out_specs=[tile_t, tile(fw), tile_t, tile(LANES_V7X), tile(gw)],
        out_shape=[shape_t, jax.ShapeDtypeStruct((B, T, fw), BF16), shape_t,
                   jax.ShapeDtypeStruct((B, T, LANES_V7X), F32),
                   jax.ShapeDtypeStruct((B, T, gw), BF16)],
        scratch_shapes=[pltpu.VMEM((SUBLANES_V7X, LANES_V7X), F32)],
        compiler_params=_params(2),
        name="even_in",
    )(x, g, wqt, wk, wvt, wf, fb, wz, lng, lnb, ws, bs)


def _fox_attn_kernel(qt_ref, k_ref, vt_ref, ct_ref, o_ref):
    hp = pl.program_id(1)
    T = k_ref.shape[0]
    tq = FOX_BLOCK
    hd = FOX_HEAD_DIM
    ct = ct_ref[...]
    lane = lax.broadcasted_iota(jnp.int32, ct.shape, 1)
    key = lax.broadcasted_iota(jnp.int32, (tq, tq), 0)
    qry = lax.broadcasted_iota(jnp.int32, (tq, tq), 1)
    causal = key <= qry
    zeros = jnp.zeros((hd, tq), BF16)
    ck = [jnp.sum(jnp.where(lane == 2 * hp + j, ct, 0.0), axis=1, keepdims=True) for j in range(2)]

    def scores(i, j):
        r0 = i * tq
        qt = qt_ref[j * hd:(j + 1) * hd, r0:r0 + tq]
        qt = jnp.concatenate([qt, zeros] if j == 0 else [zeros, qt], axis=0)
        s_d = _mm(k_ref[r0:r0 + tq, :], qt)
        s_o = _mm(k_ref[:r0, :], qt) if i > 0 else None
        return s_d, s_o

    def attend(i, j, s_d, s_o):
        r0 = i * tq
        s_d = jnp.where(causal, s_d - ck[j][r0:r0 + tq], -jnp.inf)
        m = jnp.max(s_d, axis=0, keepdims=True)
        if i > 0:
            s_o = s_o - ck[j][:r0]
            m = jnp.maximum(m, jnp.max(s_o, axis=0, keepdims=True))
        p_d = jnp.exp2(s_d - m)
        l = jnp.sum(p_d, axis=0, keepdims=True)
        acc = _mm(vt_ref[j * hd:(j + 1) * hd, r0:r0 + tq], p_d.astype(BF16))
        if i > 0:
            p_o = jnp.exp2(s_o - m)
            l = l + jnp.sum(p_o, axis=0, keepdims=True)
            acc = acc + _mm(vt_ref[j * hd:(j + 1) * hd, :r0], p_o.astype(BF16))
        return (acc * (1.0 / l)).T

    work = [(i, j) for i in reversed(range(T // tq)) for j in range(2)]
    pending = scores(*work[0])
    outs = []
    for n, (i, j) in enumerate(work):
        current = pending
        if n + 1 < len(work):
            pending = scores(*work[n + 1])
        outs.append(attend(i, j, *current))
        if j == 1:
            o_ref[i * tq:(i + 1) * tq, :] = jnp.concatenate(outs, axis=1).astype(BF16)
            outs = []


def _fox_attn(qt, k, vt, ct):
    B, T, fw = k.shape
    pair = 2 * FOX_HEAD_DIM
    spec_t = pl.BlockSpec((None, pair, T), lambda b, h: (b, h, 0))
    spec = pl.BlockSpec((None, T, pair), lambda b, h: (b, 0, h))
    return pl.pallas_call(
        _fox_attn_kernel,
        grid=(B, fw // pair),
        in_specs=[spec_t, spec, spec_t,
                  pl.BlockSpec((None, T, ct.shape[2]), lambda b, h: (b, 0, 0))],
        out_specs=spec,
        out_shape=jax.ShapeDtypeStruct((B, T, fw), BF16),
        compiler_params=_params(2),
        name="fox_attn",
    )(qt, k, vt, ct)


def _mem_kv_kernel(m_ref, g_ref, wkv_ref, kt_ref, v_ref):
    m = _rmsnorm(m_ref[...], g_ref[...]).astype(BF16)
    kv = _mm(m, wkv_ref[...])
    d = kv.shape[1] // 2
    kt_ref[...] = kv[:, :d].T.astype(BF16)
    v_ref[...] = kv[:, d:].astype(BF16)


def _mem_kv(mem, g, wkv):
    B, M, D = mem.shape
    depth = wkv.shape[0]
    return pl.pallas_call(
        _mem_kv_kernel,
        grid=(depth, B),
        in_specs=[pl.BlockSpec((None, M, D), lambda l, b: (b, 0, 0)),
                  pl.BlockSpec((None, 1, D), lambda l, b: (l, 0, 0)),
                  pl.BlockSpec((None, D, 2 * D), lambda l, b: (l, 0, 0))],
        out_specs=[pl.BlockSpec((None, None, D, M), lambda l, b: (l, b, 0, 0)),
                   pl.BlockSpec((None, None, M, D), lambda l, b: (l, b, 0, 0))],
        out_shape=[jax.ShapeDtypeStruct((depth, B, D, M), BF16),
                   jax.ShapeDtypeStruct((depth, B, M, D), BF16)],
        compiler_params=_params(2),
        name="mem_kv",
    )(mem, g, wkv)


def _xattn_body(xs, g_ref, wq_ref, kt_ref, v_ref, wo_ref):
    hd = wq_ref.shape[1] // XA_HEADS
    heads = [slice(hh * hd, (hh + 1) * hd) for hh in range(XA_HEADS)]
    hs = [_rmsnorm(x, g_ref[...]).astype(BF16) for x in xs]
    qs = [(_mm(h, wq_ref[...]) * hd ** -0.5).astype(BF16) for h in hs]
    scores = [[_mm(q[:, sl], kt_ref[sl, :]) for sl in heads] for q in qs]
    ys = []
    for x, chunk_scores in zip(xs, scores):
        outs = []
        for s, sl in zip(chunk_scores, heads):
            p = jnp.exp(s - jnp.max(s, axis=1, keepdims=True))
            p = (p / jnp.sum(p, axis=1, keepdims=True)).astype(BF16)
            outs.append(_mm(p, v_ref[:, sl]).astype(BF16))
        ys.append(x + _mm(jnp.concatenate(outs, axis=1), wo_ref[...]))
    return ys


def _row_chunks(n):
    rows = n // XA_ROW_CHUNKS
    return [slice(c * rows, (c + 1) * rows) for c in range(XA_ROW_CHUNKS)]


def _xattn_kernel(x_ref, g_ref, wq_ref, kt_ref, v_ref, wo_ref, o_ref):
    chunks = _row_chunks(x_ref.shape[0])
    ys = _xattn_body([x_ref[c, :] for c in chunks], g_ref, wq_ref, kt_ref, v_ref, wo_ref)
    for c, y in zip(chunks, ys):
        o_ref[c, :] = y


def _mix_xattn_kernel(x_ref, b_ref, a_ref, wout_ref, g_ref, wq_ref, kt_ref, v_ref, wo_ref, o_ref):
    fw = b_ref.shape[1]
    chunks = _row_chunks(x_ref.shape[0])
    xs = [x_ref[c, :] + _mm(b_ref[c, :], wout_ref[:fw, :]) + _mm(a_ref[c, :], wout_ref[fw:, :])
          for c in chunks]
    ys = _xattn_body(xs, g_ref, wq_ref, kt_ref, v_ref, wo_ref)
    for c, y in zip(chunks, ys):
        o_ref[c, :] = y


def _xattn(x, mix, g, wq, kt, v, wo):
    B, T, D = x.shape
    tm = FFN_TOKEN_TILE
    M = v.shape[1]
    tile = lambda w_: pl.BlockSpec((None, tm, w_), lambda b, t: (b, t, 0))
    kv_specs = [pl.BlockSpec((None, D, M), lambda b, t: (b, 0, 0)),
                pl.BlockSpec((None, M, D), lambda b, t: (b, 0, 0))]
    if mix is None:
        kern, pre_args, pre_specs = _xattn_kernel, (), []
    else:
        b_out, a_out, w_out = mix
        kern, pre_args = _mix_xattn_kernel, (b_out, a_out, w_out)
        pre_specs = [tile(b_out.shape[2]), tile(a_out.shape[2]), _full(w_out)]
    return pl.pallas_call(
        kern,
        grid=(B, T // tm),
        in_specs=[tile(D)] + pre_specs + [_full(g), _full(wq)] + kv_specs + [_full(wo)],
        out_specs=tile(D),
        out_shape=jax.ShapeDtypeStruct((B, T, D), F32),
        compiler_params=_params(2),
        name="xattn" if mix is None else "mix_xattn",
    )(x, *pre_args, g, wq, kt, v, wo)


def _ffn_kernel(x_ref, g_ref, wgu_ref, wd_ref, fg_ref, o_ref, *, final):
    x = x_ref[...]
    h = _rmsnorm(x, g_ref[...]).astype(BF16)
    hidden = wd_ref.shape[0]
    hc = pl.cdiv(pl.cdiv(hidden, FFN_CHUNKS), MXU_TILE_V7X) * MXU_TILE_V7X
    y = x
    for lo in range(0, hidden, hc):
        hi = min(lo + hc, hidden)
        a = jax.nn.silu(_mm(h, wgu_ref[:, lo:hi])) * _mm(h, wgu_ref[:, hidden + lo:hidden + hi])
        y = y + _mm(a.astype(BF16), wd_ref[lo:hi, :])
    if final:
        y = _rmsnorm(y, fg_ref[...])
    o_ref[...] = y


def _ffn(x, g, wgu, wd, fg, *, final):
    B, T, D = x.shape
    tm = FFN_TOKEN_TILE
    tile = pl.BlockSpec((None, tm, D), lambda b, t: (b, t, 0))
    return pl.pallas_call(
        functools.partial(_ffn_kernel, final=final),
        grid=(B, T // tm),
        in_specs=[tile, _full(g), _full(wgu), _full(wd), _full(fg)],
        out_specs=tile,
        out_shape=jax.ShapeDtypeStruct((B, T, D), F32),
        compiler_params=_params(2),
        name="ffn_final" if final else "ffn",
    )(x, g, wgu, wd, fg)


def _conv_kernel(x_ref, g_ref, win_ref, bin_ref, dww_ref, dwb_ref, lng_ref, lnb_ref, wout_ref,
                 bout_ref, o_ref, y_buf):
    x = x_ref[...]
    tm = x.shape[0]
    taps = dww_ref.shape[0]
    tiles = y_buf.shape[1]

    @pl.when(pl.program_id(1) == 0)
    def _():
        y_buf[:CONV_HALO] = jnp.zeros((CONV_HALO,) + y_buf.shape[1:], F32)

    cw = wout_ref.shape[0]
    first = CONV_HALO - (taps - 1)
    rows = tm // CONV_ROW_CHUNKS
    for c in range(CONV_ROW_CHUNKS):
        r0 = c * rows
        h = _rmsnorm(x[r0:r0 + rows], g_ref[...]).astype(BF16)
        ag = _mm(h, win_ref[...]) + bin_ref[...]
        glu = ag[:, :cw] * jax.nn.sigmoid(ag[:, cw:])
        y_buf[CONV_HALO + r0:CONV_HALO + r0 + rows] = pltpu.einshape("m(cl)->mcl", glu, c=tiles)

    for c in range(CONV_ROW_CHUNKS):
        r0 = c * rows
        acc = None
        for j in range(taps):
            term = dww_ref[j] * y_buf[first + j + r0:first + j + r0 + rows]
            acc = term if acc is None else acc + term
        acc = pltpu.einshape("mcl->m(cl)", acc)
        y = jax.nn.silu(_layernorm(acc + dwb_ref[...], lng_ref[...], lnb_ref[...]))
        o_ref[r0:r0 + rows, :] = x[r0:r0 + rows] + _mm(y.astype(BF16), wout_ref[...]) + bout_ref[...]
    y_buf[:CONV_HALO] = y_buf[tm:tm + CONV_HALO]


def _conv(x, g, win, bin_, dww, dwb, lng, lnb, wout, bout):
    B, T, D = x.shape
    tm = TOKEN_TILE
    cw = wout.shape[0]
    tiles = cw // LANES_V7X
    tiled = lambda p: p.reshape(p.shape[0], tiles, LANES_V7X)
    dww = tiled(dww)
    tile = pl.BlockSpec((None, tm, D), lambda b, t: (b, t, 0))
    return pl.pallas_call(
        _conv_kernel,
        grid=(B, T // tm),
        in_specs=[tile, _full(g), _full(win), _full(bin_), _full(dww), _full(dwb), _full(lng),
                  _full(lnb), _full(wout), _full(bout)],
        out_specs=tile,
        out_shape=jax.ShapeDtypeStruct((B, T, D), F32),
        scratch_shapes=[pltpu.VMEM((CONV_HALO + tm, tiles, LANES_V7X), F32)],
        compiler_params=_params(2),
        name="conv",
    )(x, g, win, bin_, dww, dwb, lng, lnb, wout, bout)


def kernel(x, mem, mix_norm_e, w_in_e, fox_f_bias, gmlp_ln_g, gmlp_ln_b, gmlp_w_s, gmlp_b_s, w_out_e, mix_norm_o, conv_w_in, conv_b_in, conv_dw_w, conv_dw_b, conv_ln_g, conv_ln_b, conv_w_out, conv_b_out, xa_norm, mem_norm, xa_wq, xa_wkv, xa_wo, ffn_norm, ffn_w_gu, ffn_w_down, final_norm):
    D = x.shape[2]
    depth = xa_wq.shape[0]
    heads = fox_f_bias.shape[1]
    fw = heads * FOX_HEAD_DIM
    groups = gmlp_w_s.shape[1]
    gw = groups * GMLP_GROUP_DIM
    assert 2 * GMLP_GROUP_DIM == LANES_V7X and groups % 2 == 0 and heads % 2 == 0
    assert w_in_e.shape[2] == 3 * fw + heads + 2 * gw and conv_dw_w.shape[1] <= CONV_HALO + 1

    row = lambda p: p.reshape(1, -1)
    kt, vm = _mem_kv(mem, mem_norm.reshape(depth, 1, D), xa_wkv.astype(BF16))

    for layer in range(depth):
        li = layer // 2
        if layer % 2 == 0:
            w_in = w_in_e[li]
            wf = jnp.pad(w_in[:, 3 * fw:3 * fw + heads], ((0, 0), (0, LANES_V7X - heads)))
            fb = jnp.pad(fox_f_bias[li], (0, LANES_V7X - heads)).reshape(1, -1)
            bs = jnp.repeat(gmlp_b_s[li].T, GMLP_GROUP_DIM, axis=1)
            qt, k, vt, ct, a_out = _even_in(
                x, row(mix_norm_e[li]), w_in[:, :fw].T.astype(BF16),
                w_in[:, fw:2 * fw].astype(BF16), w_in[:, 2 * fw:3 * fw].T.astype(BF16),
                wf.astype(BF16), fb, w_in[:, 3 * fw + heads:].astype(BF16), row(gmlp_ln_g[li]),
                row(gmlp_ln_b[li]), gmlp_w_s[li], bs)
            b_out = _fox_attn(qt, k, vt, ct)
            mix = (b_out, a_out, w_out_e[li].astype(BF16))
        else:
            x = _conv(x, row(mix_norm_o[li]), conv_w_in[li].astype(BF16), row(conv_b_in[li]),
                      conv_dw_w[li], row(conv_dw_b[li]), row(conv_ln_g[li]), row(conv_ln_b[li]),
                      conv_w_out[li].astype(BF16), row(conv_b_out[li]))
            mix = None
        x = _xattn(x, mix, row(xa_norm[layer]), xa_wq[layer].astype(BF16), kt[layer], vm[layer],
                   xa_wo[layer].astype(BF16))
        x = _ffn(x, row(ffn_norm[layer]), ffn_w_gu[layer].astype(BF16),
                 ffn_w_down[layer].astype(BF16), row(final_norm), final=layer == depth - 1)
    return x
```
